```python
import math
import jax, jax.numpy as jnp
from jax import lax
import numpy as np

D_MODEL = 1024
BATCH = 2
SEQ = 8192
DEPTH = 4
DEC_BATCH = 128
DEC_SEQ = 4
PAST_LEN = 8192
PAGE_SIZE = 128

HEAD_DIM = 64
MIX_WIDTH = D_MODEL
GROUP_WIDTH = MIX_WIDTH // 4
RET_HEADS = GROUP_WIDTH // HEAD_DIM
HGRN_HEADS = GROUP_WIDTH // HEAD_DIM
SWA_HEADS = GROUP_WIDTH // HEAD_DIM
SWA_KV_HEADS = SWA_HEADS // 2
SWA_GROUP = SWA_HEADS // SWA_KV_HEADS
SWA_WINDOW = 128
S5_GROUP_CH = 16
S5_GROUPS = GROUP_WIDTH // S5_GROUP_CH
S5_STATE = 64
CHUNK = 64
FFN_HIDDEN = -(-8 * D_MODEL // (3 * 256)) * 256
KV_WIDTH = SWA_KV_HEADS * HEAD_DIM
IN_SIZES = (GROUP_WIDTH,) * 4 + (GROUP_WIDTH,) + (GROUP_WIDTH,) * 4 + (GROUP_WIDTH, KV_WIDTH, KV_WIDTH)
IN_COLS = sum(IN_SIZES)
NORM_EPS = 1e-6
MASK_VALUE = -1e30

kernel_name = 'hymba_style_retention_s5_hgrn2_swa_step'


def _rmsnorm(x, w):
    xf = x.astype(jnp.float32)
    y = xf * lax.rsqrt(jnp.mean(xf * xf, axis=-1, keepdims=True) + NORM_EPS)
    return (y * w.astype(jnp.float32)).astype(x.dtype)


def _head_norm(o, w, center):
    if center:
        o = o - jnp.mean(o, axis=-1, keepdims=True)
    return o * lax.rsqrt(jnp.mean(o * o, axis=-1, keepdims=True) + NORM_EPS) * w.astype(jnp.float32)


def _chunk_len(L):
    return CHUNK if L % CHUNK == 0 else L


def _to_chunks(t, C):
    B, L = t.shape[:2]
    return jnp.moveaxis(t.reshape((B, L // C, C) + t.shape[2:]), 1, 0)


def _from_chunks(t):
    N, B, C = t.shape[:3]
    return jnp.moveaxis(t, 0, 1).reshape((B, N * C) + t.shape[3:])


def _retention(q, k, v, s0):
    L = q.shape[1]
    C = _chunk_len(L)
    log_g = jnp.log1p(-jnp.exp2(-5.0 - jnp.arange(RET_HEADS, dtype=jnp.float32)))
    idx = jnp.arange(C, dtype=jnp.float32)
    diff = idx[:, None] - idx[None, :]
    intra = jnp.where(diff >= 0, jnp.exp(jnp.maximum(diff, 0.0)[None] * log_g[:, None, None]), 0.0)
    q_dec = jnp.exp((idx[:, None] + 1.0) * log_g[None, :])[None, :, :, None]
    k_dec = jnp.exp((C - 1.0 - idx[:, None]) * log_g[None, :])[None, :, :, None]
    c_dec = jnp.exp(C * log_g)[None, :, None, None]

    def step(s, inp):
        qc, kc, vc = inp
        sc = jnp.einsum('bthk,bshk->bhts', qc, kc) * intra
        o = jnp.einsum('bhts,bshv->bthv', sc, vc) + jnp.einsum('bthk,bhkv->bthv', qc * q_dec, s)
        s = c_dec * s + jnp.einsum('bshk,bshv->bhkv', kc * k_dec, vc)
        return s, o

    s, o = lax.scan(step, s0, (_to_chunks(q, C), _to_chunks(k, C), _to_chunks(v, C)))
    return _from_chunks(o), s


def _hgrn2(q, log_f, k, v, s0):
    L = q.shape[1]
    C = _chunk_len(L)
    causal = (jnp.arange(C)[:, None] >= jnp.arange(C)[None, :])[None, :, :, None, None]

    def step(s, inp):
        qc, lfc, kc, vc = inp
        b = jnp.cumsum(lfc, axis=1)
        diff = jnp.where(causal, b[:, :, None] - b[:, None, :], 0.0)
        decay = jnp.where(causal, jnp.exp(diff), 0.0)
        sc = jnp.einsum('bthk,btshk,bshk->bhts', qc, decay, kc)
        o = jnp.einsum('bhts,bshv->bthv', sc, vc) + jnp.einsum('bthk,bhkv->bthv', qc * jnp.exp(b), s)
        b_last = b[:, -1]
        s = jnp.exp(b_last)[..., None] * s + jnp.einsum('bshk,bshv->bhkv', kc * jnp.exp(b_last[:, None] - b), vc)
        return s, o

    s, o = lax.scan(step, s0, (_to_chunks(q, C), _to_chunks(log_f, C), _to_chunks(k, C), _to_chunks(v, C)))
    return _from_chunks(o), s


def _cplx_affine_combine(e1, e2):
    a1r, a1i, b1r, b1i = e1
    a2r, a2i, b2r, b2i = e2
    return (a1r * a2r - a1i * a2i,
            a1r * a2i + a1i * a2r,
            a2r * b1r - a2i * b1i + b2r,
            a2r * b1i + a2i * b1r + b2i)


def _s5(u, x0_re, x0_im, a_re, a_im, log_step, b_re, b_im, c_re, c_im, d):
    dt = jnp.exp(log_step)[:, None]
    mag = jnp.exp(a_re * dt)
    ab_re = mag * jnp.cos(a_im * dt)
    ab_im = mag * jnp.sin(a_im * dt)
    den = a_re * a_re + a_im * a_im
    g_re = ((ab_re - 1.0) * a_re + ab_im * a_im) / den
    g_im = (ab_im * a_re - (ab_re - 1.0) * a_im) / den
    bb_re = g_re[..., None] * b_re - g_im[..., None] * b_im
    bb_im = g_re[..., None] * b_im + g_im[..., None] * b_re
    bu_re = jnp.einsum('blgc,gpc->blgp', u, bb_re)
    bu_im = jnp.einsum('blgc,gpc->blgp', u, bb_im)
    bu_re = bu_re.at[:, 0].add(ab_re * x0_re - ab_im * x0_im)
    bu_im = bu_im.at[:, 0].add(ab_re * x0_im + ab_im * x0_re)
    ar = jnp.broadcast_to(ab_re, bu_re.shape)
    ai = jnp.broadcast_to(ab_im, bu_im.shape)
    _, _, xr, xi = lax.associative_scan(_cplx_affine_combine, (ar, ai, bu_re, bu_im), axis=1)
    y = jnp.einsum('blgp,gcp->blgc', xr, c_re) - jnp.einsum('blgp,gcp->blgc', xi, c_im)
    y = y + d.reshape(S5_GROUPS, S5_GROUP_CH) * u
    return y, xr[:, -1], xi[:, -1]


def _alibi_slopes():
    h = jnp.arange(SWA_HEADS, dtype=jnp.float32) + 1.0
    return jnp.exp2(-8.0 * h / SWA_HEADS).reshape(SWA_KV_HEADS, SWA_GROUP, 1, 1)


def _sink_softmax(sc, sinks):
    m = jnp.maximum(jnp.max(sc, axis=-1, keepdims=True), sinks)
    p = jnp.exp(sc - m)
    return p / (jnp.sum(p, axis=-1, keepdims=True) + jnp.exp(sinks - m))


def _swa_prompt(q, k, v, sinks, slopes):
    B, L = q.shape[:2]
    W = SWA_WINDOW
    nb = L // W
    qb = q.reshape(B, nb, W, SWA_KV_HEADS, SWA_GROUP, HEAD_DIM)
    kb = k.reshape(B, nb, W, SWA_KV_HEADS, HEAD_DIM)
    vb = v.reshape(B, nb, W, SWA_KV_HEADS, HEAD_DIM)

    def band(t):
        prev = jnp.pad(t, ((0, 0), (1, 0), (0, 0), (0, 0), (0, 0)))[:, :-1]
        return jnp.concatenate([prev, t], axis=2)

    kk, vv = band(kb), band(vb)
    sc = jnp.einsum('bnqhgd,bnkhd->bnhgqk', qb, kk) * (HEAD_DIM ** -0.5)
    kpos = jnp.arange(2 * W)
    dist = (W + jnp.arange(W))[:, None] - kpos[None, :]
    valid = (dist >= 0) & (dist < W)
    valid = valid[None] & ((jnp.arange(nb) > 0)[:, None, None] | (kpos >= W)[None, None, :])
    sc = jnp.where(valid[None, :, None, None], sc - slopes * dist, MASK_VALUE)
    p = _sink_softmax(sc, sinks)
    o = jnp.einsum('bnhgqk,bnkhd->bnqhgd', p, vv).reshape(B, L, SWA_HEADS * HEAD_DIM)
    return o, k[:, L - W:], v[:, L - W:]


def _swa_decode(q, k, v, buf_k, buf_v, sinks, slopes):
    B, T = q.shape[:2]
    Wb = buf_k.shape[1]
    kk = jnp.concatenate([buf_k, k], axis=1)
    vv = jnp.concatenate([buf_v, v], axis=1)
    sc = jnp.einsum('bthgd,bshd->bhgts', q, kk) * (HEAD_DIM ** -0.5)
    dist = (Wb + jnp.arange(T))[:, None] - jnp.arange(Wb + T)[None, :]
    valid = (dist >= 0) & (dist < SWA_WINDOW)
    sc = jnp.where(valid, sc - slopes * dist, MASK_VALUE)
    p = _sink_softmax(sc, sinks)
    o = jnp.einsum('bhgts,bshd->bthgd', p, vv).reshape(B, T, SWA_HEADS * HEAD_DIM)
    return o, kk[:, -Wb:], vv[:, -Wb:]


def _layer(x, l, w, s_ret, s5_re, s5_im, s_hgrn, buf_k, buf_v):
    B, L, _ = x.shape
    dt = x.dtype
    f32 = jnp.float32
    h = _rmsnorm(x, w['norm_pre_mix'][l])
    z = jnp.einsum('bld,de->ble', h, w['w_in'][l]).astype(f32)
    rq, rk, rv, rg, su, hq, hf, hi, hg, aq, ak, av = jnp.split(z, np.cumsum(IN_SIZES)[:-1].tolist(), axis=-1)

    def heads(t, n):
        return t.reshape(B, L, n, HEAD_DIM)

    o_ret, s_ret = _retention(heads(rq, RET_HEADS), heads(rk, RET_HEADS) * (HEAD_DIM ** -0.5),
                              heads(rv, RET_HEADS), s_ret.astype(f32))
    o_ret = _head_norm(o_ret, w['ret_norm_w'][l], True).reshape(B, L, GROUP_WIDTH) * jax.nn.silu(rg)

    u = su.reshape(B, L, S5_GROUPS, S5_GROUP_CH)
    y, s5_re, s5_im = _s5(u, s5_re.astype(f32), s5_im.astype(f32),
                          w['s5_a_re'][l].astype(f32), w['s5_a_im'][l].astype(f32),
                          w['s5_log_step'][l].astype(f32),
                          w['s5_b_re'][l].astype(f32), w['s5_b_im'][l].astype(f32),
                          w['s5_c_re'][l].astype(f32), w['s5_c_im'][l].astype(f32),
                          w['s5_d'][l].astype(f32))
    y = jax.nn.gelu(y.reshape(B, L, GROUP_WIDTH))
    o_s5 = y * jax.nn.sigmoid(y @ w['s5_glu_w'][l].astype(f32) + w['s5_glu_b'][l].astype(f32))

    sm = jax.nn.softmax(w['hgrn_lower_bounds'].astype(f32), axis=0)
    lb = (jnp.cumsum(sm, axis=0) - sm[0])[l]
    log_f = jax.nn.log_sigmoid(hf) + jnp.log1p(lb * jnp.exp(-hf))
    k_h = (1.0 - lb) * jax.nn.sigmoid(-hf)
    o_h, s_hgrn = _hgrn2(heads(hq, HGRN_HEADS), heads(log_f, HGRN_HEADS), heads(k_h, HGRN_HEADS),
                         heads(hi, HGRN_HEADS), s_hgrn.astype(f32))
    o_h = _head_norm(o_h, w['hgrn_norm_w'][l], False).reshape(B, L, GROUP_WIDTH) * jax.nn.silu(hg)

    qa = aq.reshape(B, L, SWA_KV_HEADS, SWA_GROUP, HEAD_DIM)
    ka = heads(ak, SWA_KV_HEADS)
    va = heads(av, SWA_KV_HEADS)
    sinks = w['swa_sinks'][l].astype(f32).reshape(SWA_KV_HEADS, SWA_GROUP, 1, 1)
    slopes = _alibi_slopes()
    if buf_k is None:
        o_a, buf_k, buf_v = _swa_prompt(qa, ka, va, sinks, slopes)
    else:
        o_a, buf_k, buf_v = _swa_decode(qa, ka, va, buf_k.astype(f32), buf_v.astype(f32), sinks, slopes)

    mix = jnp.concatenate([o_ret, o_s5, o_h, o_a], axis=-1).astype(dt)
    x = x + _rmsnorm(jnp.einsum('ble,ed->bld', mix, w['w_out'][l]), w['norm_post_mix'][l]).astype(dt)

    h = _rmsnorm(x, w['norm_pre_ffn'][l])
    f = jax.nn.silu(h @ w['ffn_w_gate'][l]) * (h @ w['ffn_w_up'][l])
    x = x + _rmsnorm(f @ w['ffn_w_down'][l], w['norm_post_ffn'][l]).astype(dt)
    new = (s_ret.astype(dt), s5_re.astype(dt), s5_im.astype(dt), s_hgrn.astype(dt), buf_k.astype(dt), buf_v.astype(dt))
    return x, new


def _trunk(x, w, states):
    B = x.shape[0]
    acc = [[] for _ in range(6)]
    if states is None:
        zeros = (jnp.zeros((B, RET_HEADS, HEAD_DIM, HEAD_DIM), x.dtype),
                 jnp.zeros((B, S5_GROUPS, S5_STATE), x.dtype),
                 jnp.zeros((B, S5_GROUPS, S5_STATE), x.dtype),
                 jnp.zeros((B, HGRN_HEADS, HEAD_DIM, HEAD_DIM), x.dtype),
                 None, None)
    for l in range(DEPTH):
        st = zeros if states is None else [s[l] for s in states]
        x, new = _layer(x, l, w, *st)
        for a, n in zip(acc, new):
            a.append(n)
    return x, [jnp.stack(a) for a in acc]


def setup_inputs(seed: int = 0) -> dict:
    key = jax.random.key(seed)
    ks = jax.random.split(key, 32)
    f32 = jnp.float32

    def nrm(i, shape, scale):
        return scale * jax.random.normal(ks[i], shape, f32)

    def gain(i, shape):
        return 1.0 + 0.05 * jax.random.normal(ks[i], shape, f32)

    swa_buf = min(SWA_WINDOW, PAST_LEN)
    return {
        'x_prompt': nrm(0, (BATCH, SEQ, D_MODEL), 1.0),
        'x_sample': nrm(1, (DEC_BATCH, DEC_SEQ, D_MODEL), 1.0),
        'state_ret': nrm(2, (DEPTH, DEC_BATCH, RET_HEADS, HEAD_DIM, HEAD_DIM), 0.5),
        'state_s5_re': nrm(3, (DEPTH, DEC_BATCH, S5_GROUPS, S5_STATE), 0.1),
        'state_s5_im': nrm(4, (DEPTH, DEC_BATCH, S5_GROUPS, S5_STATE), 0.1),
        'state_hgrn': nrm(5, (DEPTH, DEC_BATCH, HGRN_HEADS, HEAD_DIM, HEAD_DIM), 0.5),
        'cache_swa_k': nrm(6, (DEPTH, DEC_BATCH, swa_buf, SWA_KV_HEADS, HEAD_DIM), 1.0),
        'cache_swa_v': nrm(7, (DEPTH, DEC_BATCH, swa_buf, SWA_KV_HEADS, HEAD_DIM), 1.0),
        'w_in': nrm(8, (DEPTH, D_MODEL, IN_COLS), D_MODEL ** -0.5),
        'w_out': nrm(9, (DEPTH, MIX_WIDTH, D_MODEL), MIX_WIDTH ** -0.5),
        'norm_pre_mix': gain(10, (DEPTH, D_MODEL)),
        'norm_post_mix': gain(11, (DEPTH, D_MODEL)),
        'norm_pre_ffn': gain(12, (DEPTH, D_MODEL)),
        'norm_post_ffn': gain(13, (DEPTH, D_MODEL)),
        'ret_norm_w': gain(14, (DEPTH, RET_HEADS, HEAD_DIM)),
        's5_a_re': -0.5 * jnp.exp(0.05 * jax.random.normal(ks[15], (DEPTH, S5_GROUPS, S5_STATE), f32)),
        's5_a_im': jnp.broadcast_to(math.pi * jnp.arange(S5_STATE, dtype=f32), (DEPTH, S5_GROUPS, S5_STATE)),
        's5_log_step': jax.random.uniform(ks[16], (DEPTH, S5_GROUPS), f32, math.log(1e-3), math.log(1e-1)),
        's5_b_re': nrm(17, (DEPTH, S5_GROUPS, S5_STATE, S5_GROUP_CH), (2 * S5_GROUP_CH) ** -0.5),
        's5_b_im': nrm(18, (DEPTH, S5_GROUPS, S5_STATE, S5_GROUP_CH), (2 * S5_GROUP_CH) ** -0.5),
        's5_c_re': nrm(19, (DEPTH, S5_GROUPS, S5_GROUP_CH, S5_STATE), (2 * S5_STATE) ** -0.5),
        's5_c_im': nrm(20, (DEPTH, S5_GROUPS, S5_GROUP_CH, S5_STATE), (2 * S5_STATE) ** -0.5),
        's5_d': nrm(21, (DEPTH, GROUP_WIDTH), 0.5),
        's5_glu_w': nrm(22, (DEPTH, GROUP_WIDTH, GROUP_WIDTH), GROUP_WIDTH ** -0.5),
        's5_glu_b': nrm(23, (DEPTH, GROUP_WIDTH), 0.01),
        'hgrn_lower_bounds': nrm(24, (DEPTH, GROUP_WIDTH), 0.1),
        'hgrn_norm_w': gain(25, (DEPTH, HGRN_HEADS, HEAD_DIM)),
        'swa_sinks': nrm(26, (DEPTH, SWA_HEADS), 0.5),
        'ffn_w_gate': nrm(27, (DEPTH, D_MODEL, FFN_HIDDEN), D_MODEL ** -0.5),
        'ffn_w_up': nrm(28, (DEPTH, D_MODEL, FFN_HIDDEN), D_MODEL ** -0.5),
        'ffn_w_down': nrm(29, (DEPTH, FFN_HIDDEN, D_MODEL), FFN_HIDDEN ** -0.5),
    }


def reference(x_prompt, x_sample, state_ret, state_s5_re, state_s5_im, state_hgrn, cache_swa_k, cache_swa_v,
              w_in, w_out, norm_pre_mix, norm_post_mix, norm_pre_ffn, norm_post_ffn, ret_norm_w,
              s5_a_re, s5_a_im, s5_log_step, s5_b_re, s5_b_im, s5_c_re, s5_c_im, s5_d, s5_glu_w, s5_glu_b,
              hgrn_lower_bounds, hgrn_norm_w, swa_sinks, ffn_w_gate, ffn_w_up, ffn_w_down):
    w = dict(w_in=w_in, w_out=w_out, norm_pre_mix=norm_pre_mix, norm_post_mix=norm_post_mix,
             norm_pre_ffn=norm_pre_ffn, norm_post_ffn=norm_post_ffn, ret_norm_w=ret_norm_w,
             s5_a_re=s5_a_re, s5_a_im=s5_a_im, s5_log_step=s5_log_step, s5_b_re=s5_b_re, s5_b_im=s5_b_im,
             s5_c_re=s5_c_re, s5_c_im=s5_c_im, s5_d=s5_d, s5_glu_w=s5_glu_w, s5_glu_b=s5_glu_b,
             hgrn_lower_bounds=hgrn_lower_bounds, hgrn_norm_w=hgrn_norm_w, swa_sinks=swa_sinks,
             ffn_w_gate=ffn_w_gate, ffn_w_up=ffn_w_up, ffn_w_down=ffn_w_down)
    y_prompt, (p_ret, p_s5_re, p_s5_im, p_hgrn, p_swa_k, p_swa_v) = _trunk(x_prompt, w, None)
    y_sample, (s_ret, s_s5_re, s_s5_im, s_hgrn, s_swa_k, s_swa_v) = _trunk(
        x_sample, w, (state_ret, state_s5_re, state_s5_im, state_hgrn, cache_swa_k, cache_swa_v))
    return (y_prompt, y_sample, p_ret, p_s5_re, p_s5_im, p_hgrn, p_swa_k, p_swa_v,
            s_ret, s_s5_re, s_s5_im, s_hgrn, s_swa_k, s_swa_v)
```

```python
import functools
import math

import numpy as np
import jax
import jax.numpy as jnp
from jax import lax
from jax.experimental import pallas as pl
from jax.experimental.pallas import tpu as pltpu

F32 = jnp.float32
BF16 = jnp.bfloat16

D_MODEL = 1024
HEAD_DIM = 64
GROUP_WIDTH = 256
N_HEADS = 4
DEPTH = 4
SWA_WINDOW = 128
S5_STATE_WIDTH = 1024
FFN_HIDDEN = 2816
FFN_CHUNK = 256
IN_COLS = 2816
NORM_EPS = 1e-6
MASK_VALUE = -1e30
K_SCALE = HEAD_DIM ** -0.5
SUB = 16
VMEM_LIMIT = 56 * 1024 * 1024

ROW_TILE = 512
MIX_TILE = 256
DEC_ROWS = 128
DEC_LEN = 4

_NT = (((1,), (1,)), ((), ()))
_TN = (((0,), (0,)), ((), ()))


def _dot(a, b):
    return jnp.dot(a, b, preferred_element_type=F32)


def _dot_nt(a, b):
    return lax.dot_general(a, b, _NT, preferred_element_type=F32)


def _dot_tn(a, b):
    return lax.dot_general(a, b, _TN, preferred_element_type=F32)


def _params(sem):
    return pltpu.CompilerParams(dimension_semantics=sem, vmem_limit_bytes=VMEM_LIMIT)


def _rms(x, w):
    return x * lax.rsqrt(jnp.mean(x * x, axis=-1, keepdims=True) + NORM_EPS) * w


def _sigmoid(x):
    return 1.0 / (1.0 + jnp.exp(-x))


def _split2(x):
    hi = x.astype(BF16)
    lo = (x - hi.astype(F32)).astype(BF16)
    return hi, lo


def _split3(x):
    hi = x.astype(BF16)
    r1 = x - hi.astype(F32)
    mid = r1.astype(BF16)
    lo = (r1 - mid.astype(F32)).astype(BF16)
    return hi, mid, lo


def _head_id(n, axis, shape):
    return lax.broadcasted_iota(jnp.int32, shape, axis) >> 6


def _block_diag(val, dtype):
    r = _head_id(GROUP_WIDTH, 0, (GROUP_WIDTH, GROUP_WIDTH))
    c = _head_id(GROUP_WIDTH, 1, (GROUP_WIDTH, GROUP_WIDTH))
    return jnp.where(r == c, val, 0.0).astype(dtype)


def _head_sum(x, ones_bd):
    hi, lo = _split2(x)
    return _dot(hi, ones_bd) + _dot(lo, ones_bd)


def _head_norm_gate(o, g, w, center):
    mean_bd = _block_diag(1.0 / HEAD_DIM, BF16)
    if center:
        o = o - _head_sum(o, mean_bd)
    var = _head_sum(o * o, mean_bd)
    return o * lax.rsqrt(var + NORM_EPS) * w * (g * _sigmoid(g))


def _prep_kernel(are_ref, aim_ref, ls_ref, bre_ref, bim_ref, lbw_ref, bbar_ref, apow_ref, lb_ref):
    l = pl.program_id(0)
    a_re = are_ref[...]
    a_im = aim_ref[...]
    dt = jnp.exp(ls_ref[...])
    mag = jnp.exp(a_re * dt)
    ab_re = mag * jnp.cos(a_im * dt)
    ab_im = mag * jnp.sin(a_im * dt)
    den = a_re * a_re + a_im * a_im
    g_re = ((ab_re - 1.0) * a_re + ab_im * a_im) / den
    g_im = (ab_im * a_re - (ab_re - 1.0) * a_im) / den
    b_re = bre_ref[...]
    b_im = bim_ref[...]
    bbar_ref[:, 0:S5_STATE_WIDTH] = (g_re * b_re - g_im * b_im).astype(BF16)
    bbar_ref[:, S5_STATE_WIDTH:] = (g_re * b_im + g_im * b_re).astype(BF16)
    p_re, p_im = ab_re, ab_im
    for r in range(8):
        apow_ref[r:r + 1, :] = p_re
        apow_ref[8 + r:9 + r, :] = p_im
        p_re, p_im = p_re * ab_re - p_im * ab_im, p_re * ab_im + p_im * ab_re
    w = lbw_ref[...]
    e = jnp.exp(w - jnp.max(w, axis=0, keepdims=True))
    sm = e / jnp.sum(e, axis=0, keepdims=True)
    row = lax.broadcasted_iota(jnp.int32, sm.shape, 0)
    lb_ref[...] = jnp.sum(jnp.where((row >= 1) & (row <= l), sm, 0.0), axis=0, keepdims=True)


def _prep(a_re, a_im, log_step, b_re_bd, b_im_bd, lower_bounds):
    vec = pl.BlockSpec((None, 1, S5_STATE_WIDTH), lambda l: (l, 0, 0))
    mat = pl.BlockSpec((None, GROUP_WIDTH, S5_STATE_WIDTH), lambda l: (l, 0, 0))
    return pl.pallas_call(
        _prep_kernel,
        grid=(DEPTH,),
        in_specs=[vec, vec, vec, mat, mat, pl.BlockSpec((DEPTH, GROUP_WIDTH), lambda l: (0, 0))],
        out_specs=[pl.BlockSpec((None, GROUP_WIDTH, 2 * S5_STATE_WIDTH), lambda l: (l, 0, 0)),
                   pl.BlockSpec((None, 16, S5_STATE_WIDTH), lambda l: (l, 0, 0)),
                   pl.BlockSpec((None, 1, GROUP_WIDTH), lambda l: (l, 0, 0))],
        out_shape=[jax.ShapeDtypeStruct((DEPTH, GROUP_WIDTH, 2 * S5_STATE_WIDTH), BF16),
                   jax.ShapeDtypeStruct((DEPTH, 16, S5_STATE_WIDTH), F32),
                   jax.ShapeDtypeStruct((DEPTH, 1, GROUP_WIDTH), F32)],
        compiler_params=_params(("arbitrary",)),
        name="prep_weights",
    )(a_re, a_im, log_step, b_re_bd, b_im_bd, lower_bounds)


def _in_kernel(x_ref, nw_ref, w_ref, zr_ref, zs_ref, zh_ref, za_ref):
    h = _rms(x_ref[...], nw_ref[...]).astype(BF16)
    zr_ref[...] = _dot(h, w_ref[:, 0:1024])
    zs_ref[...] = _dot(h, w_ref[:, 1024:1280])
    zh_ref[...] = _dot(h, w_ref[:, 1280:2304])
    za_ref[...] = _dot(h, w_ref[:, 2304:2816])


def _in_proj(x, l, W):
    n = x.shape[0]
    tm = min(ROW_TILE, n)
    widths = (1024, 256, 1024, 512)
    return pl.pallas_call(
        _in_kernel,
        grid=(n // tm,),
        in_specs=[pl.BlockSpec((tm, D_MODEL), lambda i: (i, 0)),
                  pl.BlockSpec((None, 1, D_MODEL), lambda i: (l, 0, 0)),
                  pl.BlockSpec((None, D_MODEL, IN_COLS), lambda i: (l, 0, 0))],
        out_specs=[pl.BlockSpec((tm, w), lambda i: (i, 0)) for w in widths],
        out_shape=[jax.ShapeDtypeStruct((n, w), F32) for w in widths],
        compiler_params=_params(("arbitrary",)),
        name="in_proj",
    )(x, W["norm_pre_mix"], W["w_in"])


def _out_kernel(x_ref, o1_ref, o2_ref, o3_ref, o4_ref, wo_ref, npm_ref, npf_ref, npo_ref,
                wg_ref, wu_ref, wd_ref, y_ref):
    m = _dot(o1_ref[...], wo_ref[0:256, :])
    m = m + _dot(o2_ref[...], wo_ref[256:512, :])
    m = m + _dot(o3_ref[...], wo_ref[512:768, :])
    m = m + _dot(o4_ref[...], wo_ref[768:1024, :])
    x1 = x_ref[...] + _rms(m, npm_ref[...])
    h = _rms(x1, npf_ref[...]).astype(BF16)
    acc = jnp.zeros(x1.shape, F32)
    for c in range(FFN_HIDDEN // FFN_CHUNK):
        cs = slice(c * FFN_CHUNK, (c + 1) * FFN_CHUNK)
        gate = _dot(h, wg_ref[:, cs])
        up = _dot(h, wu_ref[:, cs])
        f = (gate * _sigmoid(gate) * up).astype(BF16)
        acc = acc + _dot(f, wd_ref[cs, :])
    y_ref[...] = x1 + _rms(acc, npo_ref[...])


def _out_ffn(x, outs, l, W):
    n = x.shape[0]
    tm = min(ROW_TILE, n)
    row = lambda w: pl.BlockSpec((tm, w), lambda i: (i, 0))
    vec = pl.BlockSpec((None, 1, D_MODEL), lambda i: (l, 0, 0))
    once = dict(pipeline_mode=pl.Buffered(1))
    return pl.pallas_call(
        _out_kernel,
        grid=(n // tm,),
        in_specs=[row(D_MODEL)] + [row(GROUP_WIDTH)] * 4 + [
            pl.BlockSpec((None, D_MODEL, D_MODEL), lambda i: (l, 0, 0), **once),
            vec, vec, vec,
            pl.BlockSpec((None, D_MODEL, FFN_HIDDEN), lambda i: (l, 0, 0), **once),
            pl.BlockSpec((None, D_MODEL, FFN_HIDDEN), lambda i: (l, 0, 0), **once),
            pl.BlockSpec((None, FFN_HIDDEN, D_MODEL), lambda i: (l, 0, 0), **once)],
        out_specs=row(D_MODEL),
        out_shape=jax.ShapeDtypeStruct((n, D_MODEL), F32),
        compiler_params=_params(("arbitrary",)),
        name="out_ffn",
    )(x, *outs, W["w_out"], W["norm_post_mix"], W["norm_pre_ffn"], W["norm_post_ffn"],
      W["ffn_w_gate"], W["ffn_w_up"], W["ffn_w_down"])


def _ret_log_gamma():
    return np.log1p(-np.exp2(-5.0 - np.arange(N_HEADS, dtype=np.float64)))


def _lane_heads(v):
    return np.repeat(np.asarray(v, np.float64), HEAD_DIM)[None, :]


@functools.lru_cache(maxsize=None)
def _ret_prompt_consts(t):
    lg = _ret_log_gamma()
    idx = np.arange(t, dtype=np.float64)
    diff = idx[:, None] - idx[None, :]
    intra = np.where(diff >= 0, np.exp(np.maximum(diff, 0.0)[None] * lg[:, None, None]), 0.0)
    q_dec = np.exp((idx[:, None] + 1.0) * _lane_heads(lg))
    k_dec = np.exp((t - 1.0 - idx[:, None]) * _lane_heads(lg))
    c_dec = np.exp(t * _lane_heads(lg))
    return tuple(np.asarray(a, np.float32) for a in (intra, q_dec, k_dec, c_dec))


@functools.lru_cache(maxsize=None)
def _ret_decode_consts(rows):
    lg = _ret_log_gamma()
    t = (np.arange(rows) % DEC_LEN).astype(np.float64)
    g_pow = np.stack([np.exp(d * _lane_heads(lg)) for d in range(DEC_LEN)])
    q_dec = np.exp((t[:, None] + 1.0) * _lane_heads(lg))
    k_dec = np.exp((DEC_LEN - 1.0 - t[:, None]) * _lane_heads(lg))
    return tuple(np.asarray(a, np.float32) for a in (g_pow, q_dec, k_dec))


def _hgrn_levels(t):
    levels = []
    hs = t // 2
    while hs >= SUB:
        levels.append(hs)
        hs //= 2
    return tuple(levels)


@functools.lru_cache(maxsize=None)
def _hgrn_prompt_consts(t):
    i = np.arange(t)[:, None]
    j = np.arange(t)[None, :]
    mats = [j <= i, j > i]
    masks = []
    for hs in _hgrn_levels(t):
        same = (i // (2 * hs)) == (j // (2 * hs))
        ref = (i // (2 * hs)) * (2 * hs) + hs - 1
        upper = (i % (2 * hs)) >= hs
        mats.append(same & np.where(upper, (j > ref) & (j <= i), (j > i) & (j <= ref)))
        masks.append(same & upper & ((j % (2 * hs)) < hs))
    m = np.concatenate([a.astype(np.float32) for a in mats], axis=0)
    return jnp.asarray(m, BF16), np.stack(masks).astype(np.float32)


def _alibi_slopes():
    return np.exp2(-8.0 * (np.arange(N_HEADS, dtype=np.float64) + 1.0) / N_HEADS)


@functools.lru_cache(maxsize=None)
def _swa_prompt_bias():
    w = SWA_WINDOW
    dist = (w + np.arange(w))[:, None] - np.arange(2 * w)[None, :]
    valid = (dist >= 0) & (dist < w)
    b = np.where(valid[None], -_alibi_slopes()[:, None, None] * dist[None], MASK_VALUE)
    return np.asarray(b, np.float32)


@functools.lru_cache(maxsize=None)
def _swa_decode_bias():
    w = SWA_WINDOW
    slopes = _alibi_slopes()
    bc = np.full((2, 32, w), MASK_VALUE, np.float64)
    bn = np.full((2, 32, 8), MASK_VALUE, np.float64)
    for par in range(2):
        for g in range(2):
            for j in range(2):
                h = j * 2 + g
                for t8 in range(8):
                    t = t8 - DEC_LEN * par
                    if not 0 <= t < DEC_LEN:
                        continue
                    r = (g * 2 + j) * 8 + t8
                    s = np.arange(w)
                    dist = w + t - s
                    bc[par, r] = np.where((dist >= 0) & (dist < w), -slopes[h] * dist, MASK_VALUE)
                    for u in range(t + 1):
                        bn[par, r, u + DEC_LEN * par] = -slopes[h] * (t - u)
    return np.asarray(bc, np.float32), np.asarray(bn, np.float32)


def _ret_prompt_kernel(z_ref, intra_ref, qdec_ref, kdec_ref, cdec_ref, nw_ref, o_ref, sout_ref, s_scr):
    i = pl.program_id(1)

    @pl.when(i == 0)
    def _():
        s_scr[...] = jnp.zeros(s_scr.shape, F32)

    q = z_ref[:, 0:256]
    k = z_ref[:, 256:512] * K_SCALE
    v = z_ref[:, 512:768]
    g = z_ref[:, 768:1024]
    lane_head = _head_id(GROUP_WIDTH, 1, (1, GROUP_WIDTH))
    kb = k.astype(BF16)
    vb = v.astype(BF16)
    s_old = s_scr[...]
    o = _dot((q * qdec_ref[...]).astype(BF16), s_old.astype(BF16))
    for h in range(N_HEADS):
        mh = lane_head == h
        sc = _dot_nt(jnp.where(mh, q, 0.0).astype(BF16), kb)
        p = (sc * intra_ref[h]).astype(BF16)
        o = o + jnp.where(mh, _dot(p, vb), 0.0)
    upd = _dot_tn((k * kdec_ref[...]).astype(BF16), vb)
    s_new = cdec_ref[...] * s_old + upd * _block_diag(1.0, F32)
    s_scr[...] = s_new
    o_ref[...] = _head_norm_gate(o, g, nw_ref[...], True).astype(BF16)

    @pl.when(i == pl.num_programs(1) - 1)
    def _():
        for h in range(N_HEADS):
            hs = slice(h * HEAD_DIM, (h + 1) * HEAD_DIM)
            sout_ref[h] = s_new[hs, hs]


def _ret_prompt(zr, l, W, bsz, seq):
    t = MIX_TILE
    nt = seq // t
    intra, q_dec, k_dec, c_dec = _ret_prompt_consts(t)
    const = lambda shape: pl.BlockSpec(shape, lambda b, i: (0,) * len(shape))
    return pl.pallas_call(
        _ret_prompt_kernel,
        grid=(bsz, nt),
        in_specs=[pl.BlockSpec((t, 1024), lambda b, i: (b * nt + i, 0)),
                  const((N_HEADS, t, t)), const((t, GROUP_WIDTH)), const((t, GROUP_WIDTH)),
                  const((1, GROUP_WIDTH)),
                  pl.BlockSpec((None, 1, GROUP_WIDTH), lambda b, i: (l, 0, 0))],
        out_specs=[pl.BlockSpec((t, GROUP_WIDTH), lambda b, i: (b * nt + i, 0)),
                   pl.BlockSpec((None, N_HEADS, HEAD_DIM, HEAD_DIM), lambda b, i: (b, 0, 0, 0))],
        out_shape=[jax.ShapeDtypeStruct((bsz * seq, GROUP_WIDTH), BF16),
                   jax.ShapeDtypeStruct((bsz, N_HEADS, HEAD_DIM, HEAD_DIM), F32)],
        scratch_shapes=[pltpu.VMEM((GROUP_WIDTH, GROUP_WIDTH), F32)],
        compiler_params=_params(("arbitrary", "arbitrary")),
        name="ret_prompt",
    )(zr, intra, q_dec, k_dec, c_dec, W["ret_norm_w"])


def _hgrn_gates(hf, lb):
    log_sig = jnp.minimum(hf, 0.0) - jnp.log1p(jnp.exp(-jnp.abs(hf)))
    log_f = log_sig + jnp.log1p(lb * jnp.exp(-hf))
    k = (1.0 - lb) * (1.0 / (1.0 + jnp.exp(hf)))
    return log_f, k


def _hgrn_prompt_kernel(z_ref, lb_ref, nw_ref, cum_ref, lmask_ref, o_ref, sout_ref,
                        st_scr, kpad, vpad, lpad):
    i = pl.program_id(1)
    t = z_ref.shape[0]

    @pl.when(i == 0)
    def _():
        st_scr[...] = jnp.zeros(st_scr.shape, F32)

    q = z_ref[:, 0:256]
    v = z_ref[:, 512:768]
    g = z_ref[:, 768:1024]
    log_f, k = _hgrn_gates(z_ref[:, 256:512], lb_ref[...])
    lane_head = _head_id(GROUP_WIDTH, 1, (1, GROUP_WIDTH))
    row = lax.broadcasted_iota(jnp.int32, (t, 1), 0)
    vb = v.astype(BF16)

    cum = cum_ref[...]
    l1, l2, l3 = _split3(log_f)
    args = _dot(cum, l1) + _dot(cum, l2) + _dot(cum, l3)
    b = args[0:t]
    b_rev = args[t:2 * t]

    levels = _hgrn_levels(t)
    scores = [jnp.zeros((t, t), F32) for _ in range(N_HEADS)]
    for li, hs in enumerate(levels):
        e = jnp.exp(args[(2 + li) * t:(3 + li) * t])
        upper = (row & hs) != 0
        qt = jnp.where(upper, q * e, 0.0)
        kt = jnp.where(upper, 0.0, k * e).astype(BF16)
        lm = lmask_ref[li]
        for h in range(N_HEADS):
            sc = _dot_nt(jnp.where(lane_head == h, qt, 0.0).astype(BF16), kt)
            scores[h] = scores[h] + lm * sc

    zeros_pad = jnp.zeros((SUB, GROUP_WIDTH), F32)
    for pad, val in ((kpad, k), (vpad, v), (lpad, log_f)):
        pad[0:SUB, :] = zeros_pad
        pad[SUB:SUB + t, :] = val
    ones_bd = _block_diag(1.0, BF16)
    o = jnp.zeros((t, GROUP_WIDTH), F32)
    bd = jnp.zeros((t, GROUP_WIDTH), F32)
    for d in range(SUB):
        valid = (row & (SUB - 1)) >= d
        p = jnp.where(valid, q * kpad[pl.ds(SUB - d, t), :] * jnp.exp(bd), 0.0)
        o = o + _dot(p.astype(BF16), ones_bd) * vpad[pl.ds(SUB - d, t), :]
        if d + 1 < SUB:
            bd = bd + lpad[pl.ds(SUB - d, t), :]

    for h in range(N_HEADS):
        o = o + jnp.where(lane_head == h, _dot(scores[h].astype(BF16), vb), 0.0)

    st_old = st_scr[...]
    o = o + _dot_nt((q * jnp.exp(b)).astype(BF16), st_old.astype(BF16))
    upd = _dot_tn(vb, (k * jnp.exp(b_rev)).astype(BF16))
    st_new = st_old * jnp.exp(b[t - 1:t, :]) + upd * _block_diag(1.0, F32)
    st_scr[...] = st_new
    o_ref[...] = _head_norm_gate(o, g, nw_ref[...], False).astype(BF16)

    @pl.when(i == pl.num_programs(1) - 1)
    def _():
        for h in range(N_HEADS):
            hs = slice(h * HEAD_DIM, (h + 1) * HEAD_DIM)
            sout_ref[h] = st_new[hs, hs]


def _hgrn_prompt(zh, l, W, bsz, seq):
    t = MIX_TILE
    nt = seq // t
    cum, lmask = _hgrn_prompt_consts(t)
    const = lambda shape: pl.BlockSpec(shape, lambda b, i: (0,) * len(shape))
    vec = pl.BlockSpec((None, 1, GROUP_WIDTH), lambda b, i: (l, 0, 0))
    return pl.pallas_call(
        _hgrn_prompt_kernel,
        grid=(bsz, nt),
        in_specs=[pl.BlockSpec((t, 1024), lambda b, i: (b * nt + i, 0)), vec, vec,
                  const(cum.shape), const(lmask.shape)],
        out_specs=[pl.BlockSpec((t, GROUP_WIDTH), lambda b, i: (b * nt + i, 0)),
                   pl.BlockSpec((None, N_HEADS, HEAD_DIM, HEAD_DIM), lambda b, i: (b, 0, 0, 0))],
        out_shape=[jax.ShapeDtypeStruct((bsz * seq, GROUP_WIDTH), BF16),
                   jax.ShapeDtypeStruct((bsz, N_HEADS, HEAD_DIM, HEAD_DIM), F32)],
        scratch_shapes=[pltpu.VMEM((GROUP_WIDTH, GROUP_WIDTH), F32)]
        + [pltpu.VMEM((t + SUB, GROUP_WIDTH), F32)] * 3,
        compiler_params=_params(("arbitrary", "arbitrary")),
        name="hgrn_prompt",
    )(zh, W["hgrn_lb"], W["hgrn_norm_w"], cum, lmask)


def _rec_decode_kernel(*refs, hgrn):
    if hgrn:
        z_ref, s_ref, lb_ref, nw_ref, o_ref, sout_ref, qt_scr, os_scr, dec_scr, kpad, vpad, lpad = refs
    else:
        z_ref, s_ref, gpow_ref, qdec_ref, kdec_ref, nw_ref, o_ref, sout_ref, qt_scr, os_scr, kpad, vpad = refs
    rows = z_ref.shape[0]
    nseq = rows // DEC_LEN
    pad0 = 8
    q = z_ref[:, 0:256]
    v = z_ref[:, 512:768]
    g = z_ref[:, 768:1024]
    row = lax.broadcasted_iota(jnp.int32, (rows, 1), 0)
    tpos = row & (DEC_LEN - 1)
    zeros_pad = jnp.zeros((pad0, GROUP_WIDTH), F32)
    if hgrn:
        log_f, k = _hgrn_gates(z_ref[:, 256:512], lb_ref[...])
        pads = ((kpad, k), (vpad, v), (lpad, log_f))
    else:
        k = z_ref[:, 256:512] * K_SCALE
        pads = ((kpad, k), (vpad, v))
    for pad, val in pads:
        pad[0:pad0, :] = zeros_pad
        pad[pad0:pad0 + rows, :] = val
        pad[pad0 + rows:2 * pad0 + rows, :] = zeros_pad

    ones_bd = _block_diag(1.0, BF16)
    o = jnp.zeros((rows, GROUP_WIDTH), F32)
    bd = jnp.zeros((rows, GROUP_WIDTH), F32)
    b = jnp.zeros((rows, GROUP_WIDTH), F32)
    b_rev = jnp.zeros((rows, GROUP_WIDTH), F32)
    for d in range(DEC_LEN):
        valid = tpos >= d
        ks = kpad[pl.ds(pad0 - d, rows), :]
        vs = vpad[pl.ds(pad0 - d, rows), :]
        if hgrn:
            p = jnp.where(valid, q * ks * jnp.exp(bd), 0.0)
            o = o + _dot(p.astype(BF16), ones_bd) * vs
            ls = jnp.where(valid, lpad[pl.ds(pad0 - d, rows), :], 0.0)
            bd = bd + ls
            b = b + ls
            if d > 0:
                b_rev = b_rev + jnp.where(tpos + d < DEC_LEN, lpad[pl.ds(pad0 + d, rows), :], 0.0)
        else:
            p = jnp.where(valid, q * ks, 0.0)
            o = o + _dot(p.astype(BF16), ones_bd) * gpow_ref[d] * vs
    if hgrn:
        qt = q * jnp.exp(b)
        kt = k * jnp.exp(b_rev)
        dec_scr[...] = jnp.exp(b + b_rev)
        xmat, ymat = v, kt
    else:
        qt = q * qdec_ref[...]
        kt = k * kdec_ref[...]
        xmat, ymat = kt, v
    qt_scr[...] = qt
    os_scr[...] = jnp.zeros(os_scr.shape, F32)
    r2 = lax.broadcasted_iota(jnp.int32, (GROUP_WIDTH, GROUP_WIDTH), 0)
    c2 = lax.broadcasted_iota(jnp.int32, (GROUP_WIDTH, GROUP_WIDTH), 1)
    eye = jnp.where(r2 == c2, 1.0, 0.0).astype(BF16)
    x_t = _dot_nt(eye, xmat.astype(BF16)).astype(BF16)
    y_heads = [ymat[:, h * HEAD_DIM:(h + 1) * HEAD_DIM] for h in range(N_HEADS)]
    sub8 = lax.broadcasted_iota(jnp.int32, (8, 1), 0)
    lg = _ret_log_gamma()

    def body(s, carry):
        r8 = pl.multiple_of((s >> 1) * 8, 8)
        q8 = qt_scr[pl.ds(r8, 8), :]
        live8 = (sub8 >> 2) == (s & 1)
        live = (row >> 2) == s
        if hgrn:
            dec8 = dec_scr[pl.ds(r8, 8), :]
            dec_row = jnp.where((s & 1) == 0, dec8[0:1, :], dec8[DEC_LEN:DEC_LEN + 1, :])
        outs = []
        for h in range(N_HEADS):
            hs = slice(h * HEAD_DIM, (h + 1) * HEAD_DIM)
            st = s_ref[s, hs, :]
            a8 = q8[:, hs].astype(BF16)
            if hgrn:
                outs.append(_dot_nt(a8, st.astype(BF16)))
                dec = dec_row[:, hs]
            else:
                outs.append(_dot(a8, st.astype(BF16)))
                dec = float(np.exp(DEC_LEN * lg[h]))
            ym = jnp.where(live, y_heads[h], 0.0).astype(BF16)
            sout_ref[s, hs, :] = dec * st + _dot(x_t[hs, :], ym)
        o8 = jnp.concatenate(outs, axis=1)
        os_scr[pl.ds(r8, 8), :] = jnp.where(live8, o8, os_scr[pl.ds(r8, 8), :])
        return carry

    lax.fori_loop(0, nseq, body, 0)
    o = o + os_scr[...]
    o_ref[...] = _head_norm_gate(o, g, nw_ref[...], not hgrn).astype(BF16)


def _rec_decode(z, state, l, W, hgrn):
    rows = z.shape[0]
    rb = DEC_ROWS
    nseq = rb // DEC_LEN
    blk = lambda shape: pl.BlockSpec(shape, lambda i: (i,) + (0,) * (len(shape) - 1))
    const = lambda shape: pl.BlockSpec(shape, lambda i: (0,) * len(shape))
    vec = pl.BlockSpec((None, 1, GROUP_WIDTH), lambda i: (l, 0, 0))
    st_spec = blk((nseq, GROUP_WIDTH, HEAD_DIM))
    scratch = [pltpu.VMEM((rb, GROUP_WIDTH), F32), pltpu.VMEM((rb, GROUP_WIDTH), F32)]
    pad = pltpu.VMEM((rb + 16, GROUP_WIDTH), F32)
    if hgrn:
        in_specs = [blk((rb, 1024)), st_spec, vec, vec]
        args = (z, state, W["hgrn_lb"], W["hgrn_norm_w"])
        scratch = scratch + [pltpu.VMEM((rb, GROUP_WIDTH), F32), pad, pad, pad]
    else:
        g_pow, q_dec, k_dec = _ret_decode_consts(rb)
        in_specs = [blk((rb, 1024)), st_spec, const(g_pow.shape), const(q_dec.shape), const(k_dec.shape), vec]
        args = (z, state, g_pow, q_dec, k_dec, W["ret_norm_w"])
        scratch = scratch + [pad, pad]
    return pl.pallas_call(
        functools.partial(_rec_decode_kernel, hgrn=hgrn),
        grid=(rows // rb,),
        in_specs=in_specs,
        out_specs=[blk((rb, GROUP_WIDTH)), st_spec],
        out_shape=[jax.ShapeDtypeStruct((rows, GROUP_WIDTH), BF16),
                   jax.ShapeDtypeStruct(state.shape, F32)],
        scratch_shapes=scratch,
        compiler_params=_params(("arbitrary",)),
        name="hgrn_decode" if hgrn else "ret_decode",
    )(*args)


def _gelu_tanh(y):
    return 0.5 * y * (1.0 + jnp.tanh(math.sqrt(2.0 / math.pi) * (y + 0.044715 * (y * y * y))))


def _s5_output(xr, xi, u, cre_ref, cim_ref, d_ref, gw_ref, gb_ref):
    y = _dot(xr.astype(BF16), cre_ref[...]) - _dot(xi.astype(BF16), cim_ref[...]) + d_ref[...] * u
    y = _gelu_tanh(y)
    return y * _sigmoid(_dot(y.astype(BF16), gw_ref[...]) + gb_ref[...])


def _s5_local_scan(xr, xi, apow_ref, seg):
    sub = lax.broadcasted_iota(jnp.int32, (1, 8, 1), 1) & (seg - 1)
    d = 1
    while d < seg:
        ar = apow_ref[d - 1:d, :]
        ai = apow_ref[8 + d - 1:8 + d, :]
        keep = sub >= d
        sr = jnp.where(keep, pltpu.roll(xr, d, axis=1), 0.0)
        si = jnp.where(keep, pltpu.roll(xi, d, axis=1), 0.0)
        xr, xi = xr + (ar * sr - ai * si), xi + (ar * si + ai * sr)
        d *= 2
    return xr, xi


def _s5_prompt_kernel(u_ref, bbar_ref, apow_ref, cre_ref, cim_ref, d_ref, gw_ref, gb_ref,
                      o_ref, sre_ref, sim_ref, xre_scr, xim_scr, cr_scr, ci_scr):
    i = pl.program_id(1)
    t = u_ref.shape[0]
    w = S5_STATE_WIDTH

    @pl.when(i == 0)
    def _():
        cr_scr[...] = jnp.zeros(cr_scr.shape, F32)
        ci_scr[...] = jnp.zeros(ci_scr.shape, F32)

    u = u_ref[...]
    bu = _dot(u.astype(BF16), bbar_ref[...])
    xr, xi = _s5_local_scan(bu[:, 0:w].reshape(t // 8, 8, w), bu[:, w:].reshape(t // 8, 8, w), apow_ref, 8)
    xre_scr[...] = xr
    xim_scr[...] = xi
    pr = apow_ref[0:8, :]
    pi = apow_ref[8:16, :]

    def body(j, carry):
        cr, ci = carry
        yr = xre_scr[j] + (pr * cr - pi * ci)
        yi = xim_scr[j] + (pr * ci + pi * cr)
        xre_scr[j] = yr
        xim_scr[j] = yi
        return jnp.broadcast_to(yr[7:8, :], (8, w)), jnp.broadcast_to(yi[7:8, :], (8, w))

    cr, ci = lax.fori_loop(0, t // 8, body, (cr_scr[...], ci_scr[...]))
    cr_scr[...] = cr
    ci_scr[...] = ci
    xr = xre_scr[...].reshape(t, w)
    xi = xim_scr[...].reshape(t, w)
    o_ref[...] = _s5_output(xr, xi, u, cre_ref, cim_ref, d_ref, gw_ref, gb_ref).astype(BF16)

    @pl.when(i == pl.num_programs(1) - 1)
    def _():
        sre_ref[...] = cr[0:1, :]
        sim_ref[...] = ci[0:1, :]


def _s5_weight_specs(l, nidx):
    zero = (0,) * (nidx - 1)
    lay = lambda shape: pl.BlockSpec((None,) + shape, lambda *idx: (l, 0, 0))
    del zero
    return [lay((GROUP_WIDTH, 2 * S5_STATE_WIDTH)), lay((16, S5_STATE_WIDTH)),
            lay((S5_STATE_WIDTH, GROUP_WIDTH)), lay((S5_STATE_WIDTH, GROUP_WIDTH)),
            lay((1, GROUP_WIDTH)), lay((GROUP_WIDTH, GROUP_WIDTH)), lay((1, GROUP_WIDTH))]


def _s5_weights(W):
    return (W["s5_bbar"], W["s5_apow"], W["s5_c_re"], W["s5_c_im"], W["s5_d"], W["s5_glu_w"], W["s5_glu_b"])


def _s5_prompt(zs, l, W, bsz, seq):
    t = MIX_TILE
    nt = seq // t
    w = S5_STATE_WIDTH
    st = pl.BlockSpec((None, 1, w), lambda b, i: (b, 0, 0))
    return pl.pallas_call(
        _s5_prompt_kernel,
        grid=(bsz, nt),
        in_specs=[pl.BlockSpec((t, GROUP_WIDTH), lambda b, i: (b * nt + i, 0))] + _s5_weight_specs(l, 2),
        out_specs=[pl.BlockSpec((t, GROUP_WIDTH), lambda b, i: (b * nt + i, 0)), st, st],
        out_shape=[jax.ShapeDtypeStruct((bsz * seq, GROUP_WIDTH), BF16),
                   jax.ShapeDtypeStruct((bsz, 1, w), F32), jax.ShapeDtypeStruct((bsz, 1, w), F32)],
        scratch_shapes=[pltpu.VMEM((t // 8, 8, w), F32), pltpu.VMEM((t // 8, 8, w), F32),
                        pltpu.VMEM((8, w), F32), pltpu.VMEM((8, w), F32)],
        compiler_params=_params(("arbitrary", "arbitrary")),
        name="s5_prompt",
    )(zs, *_s5_weights(W))


def _s5_decode_kernel(u_ref, x0r_ref, x0i_ref, bbar_ref, apow_ref, cre_ref, cim_ref, d_ref, gw_ref, gb_ref,
                      o_ref, xr_ref, xi_ref):
    rows = u_ref.shape[0]
    w = S5_STATE_WIDTH
    u = u_ref[...]
    bu = _dot(u.astype(BF16), bbar_ref[...])
    first = (lax.broadcasted_iota(jnp.int32, (rows, 1), 0) & (DEC_LEN - 1)) == 0
    ar = apow_ref[0:1, :]
    ai = apow_ref[8:9, :]
    x0r = x0r_ref[...]
    x0i = x0i_ref[...]
    br = bu[:, 0:w] + jnp.where(first, ar * x0r - ai * x0i, 0.0)
    bi = bu[:, w:] + jnp.where(first, ar * x0i + ai * x0r, 0.0)
    xr, xi = _s5_local_scan(br.reshape(rows // 8, 8, w), bi.reshape(rows // 8, 8, w), apow_ref, DEC_LEN)
    xr = xr.reshape(rows, w)
    xi = xi.reshape(rows, w)
    xr_ref[...] = xr
    xi_ref[...] = xi
    o_ref[...] = _s5_output(xr, xi, u, cre_ref, cim_ref, d_ref, gw_ref, gb_ref).astype(BF16)


def _s5_decode(zs, x0r, x0i, l, W):
    rows = zs.shape[0]
    rb = min(256, rows)
    w = S5_STATE_WIDTH
    blk = lambda width: pl.BlockSpec((rb, width), lambda i: (i, 0))
    return pl.pallas_call(
        _s5_decode_kernel,
        grid=(rows // rb,),
        in_specs=[blk(GROUP_WIDTH), blk(w), blk(w)] + _s5_weight_specs(l, 1),
        out_specs=[blk(GROUP_WIDTH), blk(w), blk(w)],
        out_shape=[jax.ShapeDtypeStruct((rows, GROUP_WIDTH), BF16),
                   jax.ShapeDtypeStruct((rows, w), F32), jax.ShapeDtypeStruct((rows, w), F32)],
        compiler_params=_params(("arbitrary",)),
        name="s5_decode",
    )(zs, x0r, x0i, *_s5_weights(W))


def _swa_prompt_kernel(sink_ref, za_ref, zprev_ref, bias_ref, o_ref):
    i = pl.program_id(1)
    w = SWA_WINDOW
    nblk = za_ref.shape[0] // w
    kfull = jnp.concatenate([zprev_ref[:, 256:384], za_ref[:, 256:384]], axis=0).astype(BF16)
    vfull = jnp.concatenate([zprev_ref[:, 384:512], za_ref[:, 384:512]], axis=0).astype(BF16)
    lane_kv = lax.broadcasted_iota(jnp.int32, (1, w), 1) >> 6
    col = lax.broadcasted_iota(jnp.int32, (1, 2 * w), 1)
    no_prev = (col < w) & (i == 0)
    for n in range(nblk):
        kk = kfull[n * w:(n + 2) * w]
        vv = vfull[n * w:(n + 2) * w]
        for g in range(2):
            qg = za_ref[n * w:(n + 1) * w, g * w:(g + 1) * w]
            og = jnp.zeros((w, w), F32)
            for j in range(2):
                h = j * 2 + g
                mj = lane_kv == j
                sc = _dot_nt(jnp.where(mj, qg, 0.0).astype(BF16), kk) * K_SCALE + bias_ref[h]
                if n == 0:
                    sc = jnp.where(no_prev, MASK_VALUE, sc)
                sink = sink_ref[h]
                m = jnp.maximum(jnp.max(sc, axis=-1, keepdims=True), sink)
                p = jnp.exp(sc - m)
                den = jnp.sum(p, axis=-1, keepdims=True) + jnp.exp(sink - m)
                og = og + jnp.where(mj, _dot(p.astype(BF16), vv) / den, 0.0)
            o_ref[n * w:(n + 1) * w, g * w:(g + 1) * w] = og.astype(BF16)


def _swa_prompt(za, sinks, bsz, seq):
    t = MIX_TILE
    nt = seq // t
    per = t // SWA_WINDOW
    nb = seq // SWA_WINDOW
    bias = _swa_prompt_bias()
    return pl.pallas_call(
        _swa_prompt_kernel,
        grid=(bsz, nt),
        in_specs=[pl.BlockSpec(memory_space=pltpu.SMEM),
                  pl.BlockSpec((t, 512), lambda b, i: (b * nt + i, 0)),
                  pl.BlockSpec((SWA_WINDOW, 512), lambda b, i: (b * nb + jnp.maximum(i * per - 1, 0), 0)),
                  pl.BlockSpec(bias.shape, lambda b, i: (0, 0, 0))],
        out_specs=pl.BlockSpec((t, GROUP_WIDTH), lambda b, i: (b * nt + i, 0)),
        out_shape=jax.ShapeDtypeStruct((bsz * seq, GROUP_WIDTH), BF16),
        compiler_params=_params(("arbitrary", "arbitrary")),
        name="swa_prompt",
    )(sinks, za, za, bias)


def _swa_decode_kernel(sink_ref, za_ref, kc_ref, vc_ref, bc_ref, bn_ref, o_ref, os_scr):
    rows = za_ref.shape[0]
    nseq = rows // DEC_LEN
    w = SWA_WINDOW
    lane_kv = lax.broadcasted_iota(jnp.int32, (1, w), 1) >> 6
    sub8 = lax.broadcasted_iota(jnp.int32, (8, 1), 0)
    rgrp = lax.broadcasted_iota(jnp.int32, (32, 1), 0) >> 3
    sink = jnp.zeros((32, 1), F32)
    for g in range(2):
        for j in range(2):
            sink = jnp.where(rgrp == g * 2 + j, sink_ref[j * 2 + g], sink)
    os_scr[...] = jnp.zeros(os_scr.shape, F32)

    def body(s, carry):
        r8 = pl.multiple_of((s >> 1) * 8, 8)
        par = s & 1
        q8 = za_ref[pl.ds(r8, 8), 0:256]
        k8 = za_ref[pl.ds(r8, 8), 256:384].astype(BF16)
        v8 = za_ref[pl.ds(r8, 8), 384:512].astype(BF16)
        qs = jnp.concatenate([jnp.where(lane_kv == j, q8[:, g * w:(g + 1) * w], 0.0)
                              for g in range(2) for j in range(2)], axis=0).astype(BF16)
        sc_c = _dot_nt(qs, kc_ref[s].astype(BF16)) * K_SCALE + bc_ref[par]
        sc_n = _dot_nt(qs, k8) * K_SCALE + bn_ref[par]
        m = jnp.maximum(jnp.maximum(jnp.max(sc_c, axis=-1, keepdims=True),
                                    jnp.max(sc_n, axis=-1, keepdims=True)), sink)
        p_c = jnp.exp(sc_c - m)
        p_n = jnp.exp(sc_n - m)
        den = jnp.sum(p_c, axis=-1, keepdims=True) + jnp.sum(p_n, axis=-1, keepdims=True) + jnp.exp(sink - m)
        o32 = (_dot(p_c.astype(BF16), vc_ref[s].astype(BF16)) + _dot(p_n.astype(BF16), v8)) / den
        halves = []
        for g in range(2):
            og = jnp.zeros((8, w), F32)
            for j in range(2):
                r0 = (g * 2 + j) * 8
                og = og + jnp.where(lane_kv == j, o32[r0:r0 + 8, :], 0.0)
            halves.append(og)
        o8 = jnp.concatenate(halves, axis=1)
        live8 = (sub8 >> 2) == par
        os_scr[pl.ds(r8, 8), :] = jnp.where(live8, o8, os_scr[pl.ds(r8, 8), :])
        return carry

    lax.fori_loop(0, nseq, body, 0)
    o_ref[...] = os_scr[...].astype(BF16)


def _swa_decode(za, kc, vc, sinks):
    rows = za.shape[0]
    rb = DEC_ROWS
    nseq = rb // DEC_LEN
    bc, bn = _swa_decode_bias()
    cache = pl.BlockSpec((nseq, SWA_WINDOW, 128), lambda i: (i, 0, 0))
    return pl.pallas_call(
        _swa_decode_kernel,
        grid=(rows // rb,),
        in_specs=[pl.BlockSpec(memory_space=pltpu.SMEM),
                  pl.BlockSpec((rb, 512), lambda i: (i, 0)), cache, cache,
                  pl.BlockSpec(bc.shape, lambda i: (0, 0, 0)), pl.BlockSpec(bn.shape, lambda i: (0, 0, 0))],
        out_specs=pl.BlockSpec((rb, GROUP_WIDTH), lambda i: (i, 0)),
        out_shape=jax.ShapeDtypeStruct((rows, GROUP_WIDTH), BF16),
        scratch_shapes=[pltpu.VMEM((rb, GROUP_WIDTH), F32)],
        compiler_params=_params(("arbitrary",)),
        name="swa_decode",
    )(sinks, za, kc, vc, bc, bn)


_SWA_HEAD_ORDER = (0, 2, 1, 3)


def _swa_perm():
    return np.concatenate([np.arange(h * HEAD_DIM, (h + 1) * HEAD_DIM) for h in _SWA_HEAD_ORDER])


def _prepare_weights(w_in, w_out, norm_pre_mix, norm_post_mix, norm_pre_ffn, norm_post_ffn, ret_norm_w,
                     s5_a_re, s5_a_im, s5_log_step, s5_b_re, s5_b_im, s5_c_re, s5_c_im, s5_d, s5_glu_w,
                     s5_glu_b, hgrn_lower_bounds, hgrn_norm_w, swa_sinks, ffn_w_gate, ffn_w_up, ffn_w_down):
    perm = _swa_perm()
    in_cols = np.arange(IN_COLS)
    in_cols[2304:2560] = 2304 + perm
    out_rows = np.arange(D_MODEL)
    out_rows[768:1024] = 768 + perm
    eye = jnp.eye(16, dtype=F32)
    b_bd = lambda b: jnp.einsum("lgpc,gh->lgchp", b, eye).reshape(DEPTH, GROUP_WIDTH, S5_STATE_WIDTH)
    c_bd = lambda c: jnp.einsum("lgcp,gh->lgphc", c, eye).reshape(DEPTH, S5_STATE_WIDTH, GROUP_WIDTH)
    vec = lambda a, n: a.reshape(DEPTH, 1, n)
    bbar, apow, lb = _prep(vec(s5_a_re, S5_STATE_WIDTH), vec(s5_a_im, S5_STATE_WIDTH),
                           jnp.repeat(s5_log_step, 64, axis=1).reshape(DEPTH, 1, S5_STATE_WIDTH),
                           b_bd(s5_b_re), b_bd(s5_b_im), hgrn_lower_bounds)
    return dict(
        w_in=w_in[:, :, in_cols].astype(BF16), w_out=w_out[:, out_rows, :].astype(BF16),
        norm_pre_mix=vec(norm_pre_mix, D_MODEL), norm_post_mix=vec(norm_post_mix, D_MODEL),
        norm_pre_ffn=vec(norm_pre_ffn, D_MODEL), norm_post_ffn=vec(norm_post_ffn, D_MODEL),
        ret_norm_w=vec(ret_norm_w, GROUP_WIDTH), hgrn_norm_w=vec(hgrn_norm_w, GROUP_WIDTH), hgrn_lb=lb,
        s5_bbar=bbar, s5_apow=apow, s5_c_re=c_bd(s5_c_re).astype(BF16), s5_c_im=c_bd(s5_c_im).astype(BF16),
        s5_d=vec(s5_d, GROUP_WIDTH), s5_glu_w=s5_glu_w.astype(BF16), s5_glu_b=vec(s5_glu_b, GROUP_WIDTH),
        swa_sinks=swa_sinks,
        ffn_w_gate=ffn_w_gate.astype(BF16), ffn_w_up=ffn_w_up.astype(BF16), ffn_w_down=ffn_w_down.astype(BF16))


def _prompt_layer(x, l, W, bsz, seq):
    zr, zs, zh, za = _in_proj(x, l, W)
    o_ret, s_ret = _ret_prompt(zr, l, W, bsz, seq)
    o_s5, s_re, s_im = _s5_prompt(zs, l, W, bsz, seq)
    o_h, s_ht = _hgrn_prompt(zh, l, W, bsz, seq)
    o_a = _swa_prompt(za, W["swa_sinks"][l], bsz, seq)
    x = _out_ffn(x, (o_ret, o_s5, o_h, o_a), l, W)
    kv = za.reshape(bsz, seq, 512)[:, seq - SWA_WINDOW:, 256:]
    new = (s_ret, s_re.reshape(bsz, 16, 64), s_im.reshape(bsz, 16, 64), jnp.swapaxes(s_ht, -1, -2),
           kv[..., 0:128].reshape(bsz, SWA_WINDOW, 2, HEAD_DIM), kv[..., 128:].reshape(bsz, SWA_WINDOW, 2, HEAD_DIM))
    return x, new


def _decode_layer(x, l, W, bsz, states):
    s_ret, s5_re, s5_im, s_hgrn, buf_k, buf_v = states
    w = S5_STATE_WIDTH
    zr, zs, zh, za = _in_proj(x, l, W)
    o_ret, n_ret = _rec_decode(zr, s_ret.reshape(bsz, GROUP_WIDTH, HEAD_DIM), l, W, False)
    x0r = jnp.repeat(s5_re.reshape(bsz, w), DEC_LEN, axis=0)
    x0i = jnp.repeat(s5_im.reshape(bsz, w), DEC_LEN, axis=0)
    o_s5, xr, xi = _s5_decode(zs, x0r, x0i, l, W)
    st_h = jnp.swapaxes(s_hgrn, -1, -2).reshape(bsz, GROUP_WIDTH, HEAD_DIM)
    o_h, n_ht = _rec_decode(zh, st_h, l, W, True)
    o_a = _swa_decode(za, buf_k.reshape(bsz, SWA_WINDOW, 128), buf_v.reshape(bsz, SWA_WINDOW, 128),
                      W["swa_sinks"][l])
    x = _out_ffn(x, (o_ret, o_s5, o_h, o_a), l, W)
    kv = za.reshape(bsz, DEC_LEN, 512)
    k_new = kv[..., 256:384].reshape(bsz, DEC_LEN, 2, HEAD_DIM)
    v_new = kv[..., 384:512].reshape(bsz, DEC_LEN, 2, HEAD_DIM)
    last = lambda a: a.reshape(bsz, DEC_LEN, w)[:, DEC_LEN - 1].reshape(bsz, 16, 64)
    new = (n_ret.reshape(bsz, N_HEADS, HEAD_DIM, HEAD_DIM), last(xr), last(xi),
           jnp.swapaxes(n_ht.reshape(bsz, N_HEADS, HEAD_DIM, HEAD_DIM), -1, -2),
           jnp.concatenate([buf_k[:, DEC_LEN:], k_new], axis=1),
           jnp.concatenate([buf_v[:, DEC_LEN:], v_new], axis=1))
    return x, new


def _trunk(x, W, states):
    bsz, seq, _ = x.shape
    x = x.reshape(bsz * seq, D_MODEL)
    acc = [[] for _ in range(6)]
    for l in range(DEPTH):
        if states is None:
            x, new = _prompt_layer(x, l, W, bsz, seq)
        else:
            x, new = _decode_layer(x, l, W, bsz, [s[l] for s in states])
        for a, n in zip(acc, new):
            a.append(n)
    return x.reshape(bsz, seq, D_MODEL), [jnp.stack(a) for a in acc]


def kernel(x_prompt, x_sample, state_ret, state_s5_re, state_s5_im, state_hgrn, cache_swa_k, cache_swa_v,
           w_in, w_out, norm_pre_mix, norm_post_mix, norm_pre_ffn, norm_post_ffn, ret_norm_w,
           s5_a_re, s5_a_im, s5_log_step, s5_b_re, s5_b_im, s5_c_re, s5_c_im, s5_d, s5_glu_w, s5_glu_b,
           hgrn_lower_bounds, hgrn_norm_w, swa_sinks, ffn_w_gate, ffn_w_up, ffn_w_down):
    W = _prepare_weights(w_in, w_out, norm_pre_mix, norm_post_mix, norm_pre_ffn, norm_post_ffn, ret_norm_w,
                         s5_a_re, s5_a_im, s5_log_step, s5_b_re, s5_b_im, s5_c_re, s5_c_im, s5_d, s5_glu_w,
                         s5_glu_b, hgrn_lower_bounds, hgrn_norm_w, swa_sinks, ffn_w_gate, ffn_w_up, ffn_w_down)
    y_prompt, p_states = _trunk(x_prompt, W, None)
    y_sample, s_states = _trunk(x_sample, W, (state_ret, state_s5_re, state_s5_im, state_hgrn,
                                              cache_swa_k, cache_swa_v))
    return (y_prompt, y_sample, *p_states, *s_states)
```

```python
import functools
import math

import numpy as np
import jax
import jax.numpy as jnp
from jax import lax
from jax.experimental import pallas as pl
from jax.experimental.pallas import tpu as pltpu

F32 = jnp.float32
BF16 = jnp.bfloat16

D_MODEL = 1024
HEAD_DIM = 64
GROUP_WIDTH = 256
N_HEADS = 4
DEPTH = 4
SWA_WINDOW = 128
S5_STATE_WIDTH = 1024
FFN_HIDDEN = 2816
FFN_CHUNK = 256
IN_COLS = 2816
NORM_EPS = 1e-6
MASK_VALUE = -1e30
K_SCALE = HEAD_DIM ** -0.5
SUB = 4
VMEM_LIMIT = 56 * 1024 * 1024

ROW_TILE = 512
MIX_TILE = 256
DEC_ROWS = 128
DEC_LEN = 4
S5_POW = 32

_NT = (((1,), (1,)), ((), ()))
_TN = (((0,), (0,)), ((), ()))


def _dot(a, b):
    return jnp.dot(a, b, preferred_element_type=F32)


def _dot_nt(a, b):
    return lax.dot_general(a, b, _NT, preferred_element_type=F32)


def _dot_tn(a, b):
    return lax.dot_general(a, b, _TN, preferred_element_type=F32)


def _params(sem):
    return pltpu.CompilerParams(dimension_semantics=sem, vmem_limit_bytes=VMEM_LIMIT)


def _rms(x, w):
    return x * lax.rsqrt(jnp.mean(x * x, axis=-1, keepdims=True) + NORM_EPS) * w


def _sigmoid(x):
    return 1.0 / (1.0 + jnp.exp(-x))


def _split2(x):
    hi = x.astype(BF16)
    lo = (x - hi.astype(F32)).astype(BF16)
    return hi, lo


def _split3(x):
    hi = x.astype(BF16)
    r1 = x - hi.astype(F32)
    mid = r1.astype(BF16)
    lo = (r1 - mid.astype(F32)).astype(BF16)
    return hi, mid, lo


def _head_id(axis, shape):
    return lax.broadcasted_iota(jnp.int32, shape, axis) >> 6


def _block_diag(val, dtype):
    r = _head_id(0, (GROUP_WIDTH, GROUP_WIDTH))
    c = _head_id(1, (GROUP_WIDTH, GROUP_WIDTH))
    return jnp.where(r == c, val, 0.0).astype(dtype)


def _head_sum(x, ones_bd):
    hi, lo = _split2(x)
    return _dot(jnp.concatenate([hi, lo], axis=1), jnp.concatenate([ones_bd, ones_bd], axis=0))


def _head_norm_gate(o, g, w, center):
    mean_bd = _block_diag(1.0 / HEAD_DIM, BF16)
    if center:
        o = o - _head_sum(o, mean_bd)
    var = _head_sum(o * o, mean_bd)
    return o * lax.rsqrt(var + NORM_EPS) * w * (g * _sigmoid(g))


def _prep_kernel(are_ref, aim_ref, ls_ref, bre_ref, bim_ref, lbw_ref, bbar_ref, apow_ref, lb_ref):
    l = pl.program_id(0)
    a_re = are_ref[...]
    a_im = aim_ref[...]
    dt = jnp.exp(ls_ref[...])
    mag = jnp.exp(a_re * dt)
    ab_re = mag * jnp.cos(a_im * dt)
    ab_im = mag * jnp.sin(a_im * dt)
    den = a_re * a_re + a_im * a_im
    g_re = ((ab_re - 1.0) * a_re + ab_im * a_im) / den
    g_im = (ab_im * a_re - (ab_re - 1.0) * a_im) / den
    b_re = bre_ref[...]
    b_im = bim_ref[...]
    bbar_ref[:, 0:S5_STATE_WIDTH] = (g_re * b_re - g_im * b_im).astype(BF16)
    bbar_ref[:, S5_STATE_WIDTH:] = (g_re * b_im + g_im * b_re).astype(BF16)
    p_re, p_im = ab_re, ab_im
    for r in range(S5_POW):
        apow_ref[r:r + 1, :] = p_re
        apow_ref[S5_POW + r:S5_POW + r + 1, :] = p_im
        p_re, p_im = p_re * ab_re - p_im * ab_im, p_re * ab_im + p_im * ab_re
    w = lbw_ref[...]
    e = jnp.exp(w - jnp.max(w, axis=0, keepdims=True))
    sm = e / jnp.sum(e, axis=0, keepdims=True)
    row = lax.broadcasted_iota(jnp.int32, sm.shape, 0)
    lb_ref[...] = jnp.sum(jnp.where((row >= 1) & (row <= l), sm, 0.0), axis=0, keepdims=True)


def _prep(a_re, a_im, log_step, b_re_bd, b_im_bd, lower_bounds):
    vec = pl.BlockSpec((None, 1, S5_STATE_WIDTH), lambda l: (l, 0, 0))
    mat = pl.BlockSpec((None, GROUP_WIDTH, S5_STATE_WIDTH), lambda l: (l, 0, 0))
    return pl.pallas_call(
        _prep_kernel,
        grid=(DEPTH,),
        in_specs=[vec, vec, vec, mat, mat, pl.BlockSpec((DEPTH, GROUP_WIDTH), lambda l: (0, 0))],
        out_specs=[pl.BlockSpec((None, GROUP_WIDTH, 2 * S5_STATE_WIDTH), lambda l: (l, 0, 0)),
                   pl.BlockSpec((None, 2 * S5_POW, S5_STATE_WIDTH), lambda l: (l, 0, 0)),
                   pl.BlockSpec((None, 1, GROUP_WIDTH), lambda l: (l, 0, 0))],
        out_shape=[jax.ShapeDtypeStruct((DEPTH, GROUP_WIDTH, 2 * S5_STATE_WIDTH), BF16),
                   jax.ShapeDtypeStruct((DEPTH, 2 * S5_POW, S5_STATE_WIDTH), F32),
                   jax.ShapeDtypeStruct((DEPTH, 1, GROUP_WIDTH), F32)],
        compiler_params=_params(("arbitrary",)),
        name="prep_weights",
    )(a_re, a_im, log_step, b_re_bd, b_im_bd, lower_bounds)


def _in_kernel(x_ref, nw_ref, w_ref, zr_ref, zs_ref, zh_ref, za_ref):
    h = _rms(x_ref[...], nw_ref[...]).astype(BF16)
    zr_ref[...] = _dot(h, w_ref[:, 0:1024])
    zs_ref[...] = _dot(h, w_ref[:, 1024:1280])
    zh_ref[...] = _dot(h, w_ref[:, 1280:2304])
    za_ref[...] = _dot(h, w_ref[:, 2304:2816])


def _in_proj(x, l, W):
    n = x.shape[0]
    tm = min(ROW_TILE, n)
    widths = (1024, 256, 1024, 512)
    return pl.pallas_call(
        _in_kernel,
        grid=(n // tm,),
        in_specs=[pl.BlockSpec((tm, D_MODEL), lambda i: (i, 0)),
                  pl.BlockSpec((None, 1, D_MODEL), lambda i: (l, 0, 0)),
                  pl.BlockSpec((None, D_MODEL, IN_COLS), lambda i: (l, 0, 0))],
        out_specs=[pl.BlockSpec((tm, w), lambda i: (i, 0)) for w in widths],
        out_shape=[jax.ShapeDtypeStruct((n, w), F32) for w in widths],
        compiler_params=_params(("arbitrary",)),
        name="in_proj",
    )(x, W["norm_pre_mix"], W["w_in"])


def _out_kernel(x_ref, o1_ref, o2_ref, o3_ref, o4_ref, wo_ref, npm_ref, npf_ref, npo_ref,
                wg_ref, wu_ref, wd_ref, y_ref):
    m = _dot(o1_ref[...], wo_ref[0:256, :])
    m = m + _dot(o2_ref[...], wo_ref[256:512, :])
    m = m + _dot(o3_ref[...], wo_ref[512:768, :])
    m = m + _dot(o4_ref[...], wo_ref[768:1024, :])
    x1 = x_ref[...] + _rms(m, npm_ref[...])
    h = _rms(x1, npf_ref[...]).astype(BF16)
    acc = jnp.zeros(x1.shape, F32)
    for c in range(FFN_HIDDEN // FFN_CHUNK):
        cs = slice(c * FFN_CHUNK, (c + 1) * FFN_CHUNK)
        gate = _dot(h, wg_ref[:, cs])
        up = _dot(h, wu_ref[:, cs])
        f = (gate * _sigmoid(gate) * up).astype(BF16)
        acc = acc + _dot(f, wd_ref[cs, :])
    y_ref[...] = x1 + _rms(acc, npo_ref[...])


def _out_ffn(x, outs, l, W):
    n = x.shape[0]
    tm = min(ROW_TILE, n)
    row = lambda w: pl.BlockSpec((tm, w), lambda i: (i, 0))
    vec = pl.BlockSpec((None, 1, D_MODEL), lambda i: (l, 0, 0))
    once = dict(pipeline_mode=pl.Buffered(1))
    return pl.pallas_call(
        _out_kernel,
        grid=(n // tm,),
        in_specs=[row(D_MODEL)] + [row(GROUP_WIDTH)] * 4 + [
            pl.BlockSpec((None, D_MODEL, D_MODEL), lambda i: (l, 0, 0), **once),
            vec, vec, vec,
            pl.BlockSpec((None, D_MODEL, FFN_HIDDEN), lambda i: (l, 0, 0), **once),
            pl.BlockSpec((None, D_MODEL, FFN_HIDDEN), lambda i: (l, 0, 0), **once),
            pl.BlockSpec((None, FFN_HIDDEN, D_MODEL), lambda i: (l, 0, 0), **once)],
        out_specs=row(D_MODEL),
        out_shape=jax.ShapeDtypeStruct((n, D_MODEL), F32),
        compiler_params=_params(("arbitrary",)),
        name="out_ffn",
    )(x, *outs, W["w_out"], W["norm_post_mix"], W["norm_pre_ffn"], W["norm_post_ffn"],
      W["ffn_w_gate"], W["ffn_w_up"], W["ffn_w_down"])


def _ret_log_gamma():
    return np.log1p(-np.exp2(-5.0 - np.arange(N_HEADS, dtype=np.float64)))


def _lane_heads(v):
    return np.repeat(np.asarray(v, np.float64), HEAD_DIM)[None, :]


@functools.lru_cache(maxsize=None)
def _ret_prompt_consts(t):
    lg = _ret_log_gamma()
    idx = np.arange(t, dtype=np.float64)
    diff = idx[:, None] - idx[None, :]
    intra = np.where(diff >= 0, np.exp(np.maximum(diff, 0.0)[None] * lg[:, None, None]), 0.0)
    q_dec = np.exp((idx[:, None] + 1.0) * _lane_heads(lg))
    k_dec = np.exp((t - 1.0 - idx[:, None]) * _lane_heads(lg))
    c_dec = np.exp(t * _lane_heads(lg))
    return tuple(np.asarray(a, np.float32) for a in (intra, q_dec, k_dec, c_dec))


@functools.lru_cache(maxsize=None)
def _ret_decode_consts(rows):
    lg = _ret_log_gamma()
    t = (np.arange(rows) % DEC_LEN).astype(np.float64)
    g_pow = np.stack([np.exp(d * _lane_heads(lg)) for d in range(DEC_LEN)])
    q_dec = np.exp((t[:, None] + 1.0) * _lane_heads(lg))
    k_dec = np.exp((DEC_LEN - 1.0 - t[:, None]) * _lane_heads(lg))
    return tuple(np.asarray(a, np.float32) for a in (g_pow, q_dec, k_dec))


def _hgrn_levels(t):
    levels = []
    hs = t // 2
    while hs >= SUB:
        levels.append(hs)
        hs //= 2
    return tuple(levels)


@functools.lru_cache(maxsize=None)
def _hgrn_prompt_consts(t):
    i = np.arange(t)[:, None]
    j = np.arange(t)[None, :]
    tri = (j <= i).astype(np.float32).astype(jnp.bfloat16)
    half = t // 2
    ih, jh = i[:half], j[:, :half]
    masks = []
    for hs in _hgrn_levels(t)[1:]:
        same = (ih // (2 * hs)) == (jh // (2 * hs))
        masks.append(same & ((ih % (2 * hs)) >= hs) & ((jh % (2 * hs)) < hs))
    return tri, np.stack(masks).astype(np.float32)


def _alibi_slopes():
    return np.exp2(-8.0 * (np.arange(N_HEADS, dtype=np.float64) + 1.0) / N_HEADS)


@functools.lru_cache(maxsize=None)
def _swa_prompt_bias():
    w = SWA_WINDOW
    dist = (w + np.arange(w))[:, None] - np.arange(2 * w)[None, :]
    valid = (dist >= 0) & (dist < w)
    b = np.where(valid[None], -_alibi_slopes()[:, None, None] * dist[None], MASK_VALUE)
    return np.asarray(b, np.float32)


@functools.lru_cache(maxsize=None)
def _swa_decode_bias():
    w = SWA_WINDOW
    slopes = _alibi_slopes()
    bc = np.full((2, 32, w), MASK_VALUE, np.float64)
    bn = np.full((2, 32, 8), MASK_VALUE, np.float64)
    for par in range(2):
        for h in range(N_HEADS):
            for t8 in range(8):
                t = t8 - DEC_LEN * par
                if not 0 <= t < DEC_LEN:
                    continue
                r = h * 8 + t8
                dist = w + t - np.arange(w)
                bc[par, r] = np.where((dist >= 0) & (dist < w), -slopes[h] * dist, MASK_VALUE)
                for u in range(t + 1):
                    bn[par, r, u + DEC_LEN * par] = -slopes[h] * (t - u)
    return np.asarray(bc, np.float32), np.asarray(bn, np.float32)


def _ret_prompt_kernel(z_ref, intra_ref, qdec_ref, kdec_ref, cdec_ref, nw_ref, o_ref, sout_ref, s_scr):
    i = pl.program_id(1)

    @pl.when(i == 0)
    def _():
        s_scr[...] = jnp.zeros(s_scr.shape, F32)

    q = z_ref[:, 0:256]
    k = z_ref[:, 256:512] * K_SCALE
    v = z_ref[:, 512:768]
    g = z_ref[:, 768:1024]
    lane_head = _head_id(1, (1, GROUP_WIDTH))
    kb = k.astype(BF16)
    vb = v.astype(BF16)
    s_old = s_scr[...]
    o = _dot((q * qdec_ref[...]).astype(BF16), s_old.astype(BF16))
    for h in range(N_HEADS):
        mh = lane_head == h
        sc = _dot_nt(jnp.where(mh, q, 0.0).astype(BF16), kb)
        p = (sc * intra_ref[h]).astype(BF16)
        o = o + jnp.where(mh, _dot(p, vb), 0.0)
    upd = _dot_tn((k * kdec_ref[...]).astype(BF16), vb)
    s_new = cdec_ref[...] * s_old + upd * _block_diag(1.0, F32)
    s_scr[...] = s_new
    o_ref[...] = _head_norm_gate(o, g, nw_ref[...], True).astype(BF16)

    @pl.when(i == pl.num_programs(1) - 1)
    def _():
        for h in range(N_HEADS):
            hs = slice(h * HEAD_DIM, (h + 1) * HEAD_DIM)
            sout_ref[h] = s_new[hs, hs]


def _ret_prompt(zr, l, W, bsz, seq):
    t = MIX_TILE
    nt = seq // t
    intra, q_dec, k_dec, c_dec = _ret_prompt_consts(t)
    const = lambda shape: pl.BlockSpec(shape, lambda b, i: (0,) * len(shape))
    return pl.pallas_call(
        _ret_prompt_kernel,
        grid=(bsz, nt),
        in_specs=[pl.BlockSpec((t, 1024), lambda b, i: (b * nt + i, 0)),
                  const((N_HEADS, t, t)), const((t, GROUP_WIDTH)), const((t, GROUP_WIDTH)),
                  const((1, GROUP_WIDTH)),
                  pl.BlockSpec((None, 1, GROUP_WIDTH), lambda b, i: (l, 0, 0))],
        out_specs=[pl.BlockSpec((t, GROUP_WIDTH), lambda b, i: (b * nt + i, 0)),
                   pl.BlockSpec((None, N_HEADS, HEAD_DIM, HEAD_DIM), lambda b, i: (b, 0, 0, 0))],
        out_shape=[jax.ShapeDtypeStruct((bsz * seq, GROUP_WIDTH), BF16),
                   jax.ShapeDtypeStruct((bsz, N_HEADS, HEAD_DIM, HEAD_DIM), F32)],
        scratch_shapes=[pltpu.VMEM((GROUP_WIDTH, GROUP_WIDTH), F32)],
        compiler_params=_params(("arbitrary", "arbitrary")),
        name="ret_prompt",
    )(zr, intra, q_dec, k_dec, c_dec, W["ret_norm_w"])


def _hgrn_gates(hf, lb):
    en = jnp.exp(-hf)
    r = 1.0 / (1.0 + en)
    log_f = jnp.log(r) + jnp.log1p(lb * en)
    k = (1.0 - lb) * (en * r)
    return log_f, k


def _hgrn_prompt_kernel(z_ref, lb_ref, nw_ref, tri_ref, lmask_ref, o_ref, sout_ref, st_scr):
    i = pl.program_id(1)
    t = z_ref.shape[0]
    half = t // 2
    w = GROUP_WIDTH

    @pl.when(i == 0)
    def _():
        st_scr[...] = jnp.zeros(st_scr.shape, F32)

    q = z_ref[:, 0:256]
    v = z_ref[:, 512:768]
    g = z_ref[:, 768:1024]
    log_f, k = _hgrn_gates(z_ref[:, 256:512], lb_ref[...])
    lane_head = _head_id(1, (1, w))
    vb = v.astype(BF16)

    tri = tri_ref[...]
    b = _dot(jnp.concatenate([tri, tri, tri], axis=1), jnp.concatenate(_split3(log_f), axis=0))

    levels = _hgrn_levels(t)
    diag_sc = [[jnp.zeros((half, half), F32) for _ in range(N_HEADS)] for _ in range(2)]
    cross_sc = [None] * N_HEADS
    for li, hs in enumerate(levels):
        nb = t // (2 * hs)
        ref = jnp.broadcast_to(b.reshape(nb, 2 * hs, w)[:, hs - 1:hs, :], (nb, 2 * hs, w)).reshape(t, w)
        e = jnp.exp(-jnp.abs(b - ref))
        qe = (q * e).astype(BF16)
        ke = (k * e).astype(BF16)
        for h in range(N_HEADS):
            mh = lane_head == h
            if li == 0:
                cross_sc[h] = _dot_nt(jnp.where(mh, qe[half:], 0), ke[:half])
            else:
                lm = lmask_ref[li - 1]
                for hf in range(2):
                    rows = slice(hf * half, (hf + 1) * half)
                    diag_sc[hf][h] = diag_sc[hf][h] + lm * _dot_nt(jnp.where(mh, qe[rows], 0), ke[rows])

    ones_bd = _block_diag(1.0, BF16)
    g3 = (t // 8, 8, w)
    sub = lax.broadcasted_iota(jnp.int32, (1, 8, 1), 1) & (SUB - 1)
    q3, k3, v3, l3 = (a.reshape(g3) for a in (q, k, v, log_f))
    o3 = jnp.zeros(g3, F32)
    bd = jnp.zeros(g3, F32)
    for d in range(SUB):
        ks = k3 if d == 0 else pltpu.roll(k3, d, axis=1)
        vs = v3 if d == 0 else pltpu.roll(v3, d, axis=1)
        p = jnp.where(sub >= d, q3 * ks * jnp.exp(bd), 0.0)
        o3 = o3 + _dot(p.reshape(t, w).astype(BF16), ones_bd).reshape(g3) * vs
        if d + 1 < SUB:
            bd = bd + (l3 if d == 0 else pltpu.roll(l3, d, axis=1))
    o = o3.reshape(t, w)

    o_lo = jnp.zeros((half, w), F32)
    o_hi = jnp.zeros((half, w), F32)
    for h in range(N_HEADS):
        mh = lane_head == h
        o_lo = o_lo + jnp.where(mh, _dot(diag_sc[0][h].astype(BF16), vb[:half]), 0.0)
        p_hi = jnp.concatenate([cross_sc[h], diag_sc[1][h]], axis=1).astype(BF16)
        o_hi = o_hi + jnp.where(mh, _dot(p_hi, vb), 0.0)
    o = o + jnp.concatenate([o_lo, o_hi], axis=0)

    st_old = st_scr[...]
    b_last = b[t - 1:t, :]
    o = o + _dot_nt((q * jnp.exp(b)).astype(BF16), st_old.astype(BF16))
    upd = _dot_tn(vb, (k * jnp.exp(b_last - b)).astype(BF16))
    st_new = st_old * jnp.exp(b_last) + upd * _block_diag(1.0, F32)
    st_scr[...] = st_new
    o_ref[...] = _head_norm_gate(o, g, nw_ref[...], False).astype(BF16)

    @pl.when(i == pl.num_programs(1) - 1)
    def _():
        for h in range(N_HEADS):
            hs = slice(h * HEAD_DIM, (h + 1) * HEAD_DIM)
            sout_ref[h] = st_new[hs, hs]


def _hgrn_prompt(zh, l, W, bsz, seq):
    t = MIX_TILE
    nt = seq // t
    tri, lmask = _hgrn_prompt_consts(t)
    const = lambda shape: pl.BlockSpec(shape, lambda b, i: (0,) * len(shape))
    vec = pl.BlockSpec((None, 1, GROUP_WIDTH), lambda b, i: (l, 0, 0))
    return pl.pallas_call(
        _hgrn_prompt_kernel,
        grid=(bsz, nt),
        in_specs=[pl.BlockSpec((t, 1024), lambda b, i: (b * nt + i, 0)), vec, vec,
                  const(tri.shape), const(lmask.shape)],
        out_specs=[pl.BlockSpec((t, GROUP_WIDTH), lambda b, i: (b * nt + i, 0)),
                   pl.BlockSpec((None, N_HEADS, HEAD_DIM, HEAD_DIM), lambda b, i: (b, 0, 0, 0))],
        out_shape=[jax.ShapeDtypeStruct((bsz * seq, GROUP_WIDTH), BF16),
                   jax.ShapeDtypeStruct((bsz, N_HEADS, HEAD_DIM, HEAD_DIM), F32)],
        scratch_shapes=[pltpu.VMEM((GROUP_WIDTH, GROUP_WIDTH), F32)],
        compiler_params=_params(("arbitrary", "arbitrary")),
        name="hgrn_prompt",
    )(zh, W["hgrn_lb"], W["hgrn_norm_w"], tri, lmask)


def _rec_decode_kernel(*refs, hgrn, aliased):
    n_in = 4 if hgrn else 6
    if hgrn:
        z_ref, s_ref, lb_ref, nw_ref = refs[:n_in]
    else:
        z_ref, s_ref, gpow_ref, qdec_ref, kdec_ref, nw_ref = refs[:n_in]
    o_ref, sout_ref, qt_scr, os_scr, *more = refs[n_in + (1 if aliased else 0):]
    rows = z_ref.shape[0]
    nseq = rows // DEC_LEN
    w = GROUP_WIDTH
    q = z_ref[:, 0:256]
    v = z_ref[:, 512:768]
    g = z_ref[:, 768:1024]
    row = lax.broadcasted_iota(jnp.int32, (rows, 1), 0)
    if hgrn:
        log_f, k = _hgrn_gates(z_ref[:, 256:512], lb_ref[...])
    else:
        k = z_ref[:, 256:512] * K_SCALE

    ones_bd = _block_diag(1.0, BF16)
    g3 = (rows // 8, 8, w)
    tpos = lax.broadcasted_iota(jnp.int32, (1, 8, 1), 1) & (DEC_LEN - 1)
    q3, k3, v3 = (a.reshape(g3) for a in (q, k, v))
    o3 = jnp.zeros(g3, F32)
    if hgrn:
        l3 = log_f.reshape(g3)
        bd = jnp.zeros(g3, F32)
        b_rev = jnp.zeros(g3, F32)
    for d in range(DEC_LEN):
        valid = tpos >= d
        ks = k3 if d == 0 else pltpu.roll(k3, d, axis=1)
        vs = v3 if d == 0 else pltpu.roll(v3, d, axis=1)
        if hgrn:
            p = jnp.where(valid, q3 * ks * jnp.exp(bd), 0.0)
            o3 = o3 + _dot(p.reshape(rows, w).astype(BF16), ones_bd).reshape(g3) * vs
            bd = bd + jnp.where(valid, l3 if d == 0 else pltpu.roll(l3, d, axis=1), 0.0)
            if d > 0:
                b_rev = b_rev + jnp.where(tpos + d < DEC_LEN, pltpu.roll(l3, 8 - d, axis=1), 0.0)
        else:
            p = jnp.where(valid, q3 * ks, 0.0)
            o3 = o3 + _dot(p.reshape(rows, w).astype(BF16), ones_bd).reshape(g3) * gpow_ref[d] * vs
    o = o3.reshape(rows, w)
    if hgrn:
        dec_scr = more[0]
        b = bd.reshape(rows, w)
        b_rev = b_rev.reshape(rows, w)
        qt = q * jnp.exp(b)
        kt = k * jnp.exp(b_rev)
        dec_scr[...] = jnp.exp(b + b_rev)
        xmat, ymat = v, kt
    else:
        qt = q * qdec_ref[...]
        kt = k * kdec_ref[...]
        xmat, ymat = kt, v
    qt_scr[...] = qt
    os_scr[...] = jnp.zeros(os_scr.shape, F32)
    r2 = lax.broadcasted_iota(jnp.int32, (w, w), 0)
    c2 = lax.broadcasted_iota(jnp.int32, (w, w), 1)
    eye = jnp.where(r2 == c2, 1.0, 0.0).astype(BF16)
    x_t = _dot_nt(eye, xmat.astype(BF16)).astype(BF16)
    y_heads = [ymat[:, h * HEAD_DIM:(h + 1) * HEAD_DIM] for h in range(N_HEADS)]
    sub8 = lax.broadcasted_iota(jnp.int32, (8, 1), 0)
    lg = _ret_log_gamma()

    def body(s, carry):
        r8 = pl.multiple_of((s >> 1) * 8, 8)
        q8 = qt_scr[pl.ds(r8, 8), :]
        live8 = (sub8 >> 2) == (s & 1)
        live = (row >> 2) == s
        if hgrn:
            dec8 = dec_scr[pl.ds(r8, 8), :]
            dec_row = jnp.where((s & 1) == 0, dec8[0:1, :], dec8[DEC_LEN:DEC_LEN + 1, :])
        outs = []
        for h in range(N_HEADS):
            hs = slice(h * HEAD_DIM, (h + 1) * HEAD_DIM)
            st = s_ref[s, hs, :]
            a8 = q8[:, hs].astype(BF16)
            if hgrn:
                outs.append(_dot_nt(a8, st.astype(BF16)))
                dec = dec_row[:, hs]
            else:
                outs.append(_dot(a8, st.astype(BF16)))
                dec = float(np.exp(DEC_LEN * lg[h]))
            ym = jnp.where(live, y_heads[h], 0.0).astype(BF16)
            sout_ref[s, hs, :] = dec * st + _dot(x_t[hs, :], ym)
        o8 = jnp.concatenate(outs, axis=1)
        os_scr[pl.ds(r8, 8), :] = jnp.where(live8, o8, os_scr[pl.ds(r8, 8), :])
        return carry

    lax.fori_loop(0, nseq, body, 0)
    o = o + os_scr[...]
    o_ref[...] = _head_norm_gate(o, g, nw_ref[...], not hgrn).astype(BF16)


def _rec_decode(z, states, prev, l, W, hgrn):
    rows = z.shape[0]
    rb = DEC_ROWS
    nseq = rb // DEC_LEN
    blk = lambda shape: pl.BlockSpec(shape, lambda i: (i,) + (0,) * (len(shape) - 1))
    const = lambda shape: pl.BlockSpec(shape, lambda i: (0,) * len(shape))
    vec = pl.BlockSpec((None, 1, GROUP_WIDTH), lambda i: (l, 0, 0))
    st_spec = pl.BlockSpec((None, nseq, GROUP_WIDTH, HEAD_DIM), lambda i: (l, i, 0, 0))
    scratch = [pltpu.VMEM((rb, GROUP_WIDTH), F32), pltpu.VMEM((rb, GROUP_WIDTH), F32)]
    if hgrn:
        in_specs = [blk((rb, 1024)), st_spec, vec, vec]
        args = [z, states, W["hgrn_lb"], W["hgrn_norm_w"]]
        scratch = scratch + [pltpu.VMEM((rb, GROUP_WIDTH), F32)]
    else:
        g_pow, q_dec, k_dec = _ret_decode_consts(rb)
        in_specs = [blk((rb, 1024)), st_spec, const(g_pow.shape), const(q_dec.shape), const(k_dec.shape), vec]
        args = [z, states, g_pow, q_dec, k_dec, W["ret_norm_w"]]
    aliases = {}
    if prev is not None:
        aliases = {len(args): 1}
        in_specs = in_specs + [pl.BlockSpec(memory_space=pl.ANY)]
        args = args + [prev]
    return pl.pallas_call(
        functools.partial(_rec_decode_kernel, hgrn=hgrn, aliased=prev is not None),
        grid=(rows // rb,),
        in_specs=in_specs,
        out_specs=[blk((rb, GROUP_WIDTH)), st_spec],
        out_shape=[jax.ShapeDtypeStruct((rows, GROUP_WIDTH), BF16),
                   jax.ShapeDtypeStruct(states.shape, F32)],
        input_output_aliases=aliases,
        scratch_shapes=scratch,
        compiler_params=_params(("arbitrary",)),
        name="hgrn_decode" if hgrn else "ret_decode",
    )(*args)


def _gelu_tanh(y):
    return 0.5 * y * (1.0 + jnp.tanh(math.sqrt(2.0 / math.pi) * (y + 0.044715 * (y * y * y))))


def _s5_output(xr, xi, u, cre_ref, cim_ref, d_ref, gw_ref, gb_ref):
    y = _dot(xr.astype(BF16), cre_ref[...]) - _dot(xi.astype(BF16), cim_ref[...]) + d_ref[...] * u
    y = _gelu_tanh(y)
    return y * _sigmoid(_dot(y.astype(BF16), gw_ref[...]) + gb_ref[...])


def _s5_local_scan(xr, xi, apow_ref, seg):
    sub = lax.broadcasted_iota(jnp.int32, (1, 8, 1), 1) & (seg - 1)
    d = 1
    while d < seg:
        ar = apow_ref[d - 1:d, :]
        ai = apow_ref[S5_POW + d - 1:S5_POW + d, :]
        keep = sub >= d
        sr = jnp.where(keep, pltpu.roll(xr, d, axis=1), 0.0)
        si = jnp.where(keep, pltpu.roll(xi, d, axis=1), 0.0)
        xr, xi = xr + (ar * sr - ai * si), xi + (ar * si + ai * sr)
        d *= 2
    return xr, xi


def _cmul(ar, ai, br, bi):
    return ar * br - ai * bi, ar * bi + ai * br


def _s5_prompt_kernel(u_ref, bbar_ref, apow_ref, cre_ref, cim_ref, d_ref, gw_ref, gb_ref, perm_ref,
                      o_ref, sre_ref, sim_ref, xre_scr, xim_scr, cr_scr, ci_scr):
    i = pl.program_id(1)
    t = u_ref.shape[0] // 2
    w = S5_STATE_WIDTH
    n = S5_POW
    assert t == 8 * n

    @pl.when(i == 0)
    def _():
        cr_scr[...] = jnp.zeros(cr_scr.shape, F32)
        ci_scr[...] = jnp.zeros(ci_scr.shape, F32)

    u = jnp.concatenate(
        [jnp.concatenate([u_ref[pl.ds(2 * j + half, 8, stride=2 * n), :] for j in range(n)], axis=0)
         for half in range(2)], axis=1)
    bu = _dot(u.astype(BF16), bbar_ref[...])
    a_re = apow_ref[0:1, :]
    a_im = apow_ref[n:n + 1, :]

    xr = jnp.zeros((8, w), F32)
    xi = jnp.zeros((8, w), F32)
    for j in range(n):
        pr, pi = _cmul(a_re, a_im, xr, xi)
        xr = pr + bu[j * 8:(j + 1) * 8, 0:w]
        xi = pi + bu[j * 8:(j + 1) * 8, w:]
        xre_scr[j * 8:(j + 1) * 8, :] = xr
        xim_scr[j * 8:(j + 1) * 8, :] = xi

    an_re = apow_ref[n - 1:n, :]
    an_im = apow_ref[2 * n - 1:2 * n, :]
    sub = lax.broadcasted_iota(jnp.int32, (8, 1), 0)
    sr = cr_scr[...]
    si = ci_scr[...]
    in_re = jnp.zeros((8, w), F32)
    in_im = jnp.zeros((8, w), F32)
    for c in range(8):
        in_re = jnp.where(sub == c, sr, in_re)
        in_im = jnp.where(sub == c, si, in_im)
        pr, pi = _cmul(an_re, an_im, sr, si)
        sr = pr + xr[c:c + 1, :]
        si = pi + xi[c:c + 1, :]
    cr_scr[...] = sr
    ci_scr[...] = si

    for j in range(n):
        pr, pi = _cmul(apow_ref[j:j + 1, :], apow_ref[n + j:n + j + 1, :], in_re, in_im)
        xre_scr[j * 8:(j + 1) * 8, :] = xre_scr[j * 8:(j + 1) * 8, :] + pr
        xim_scr[j * 8:(j + 1) * 8, :] = xim_scr[j * 8:(j + 1) * 8, :] + pi

    y = _s5_output(xre_scr[...], xim_scr[...], u, cre_ref, cim_ref, d_ref, gw_ref, gb_ref)
    o_ref[...] = _dot(perm_ref[...], y.astype(BF16)).astype(BF16)

    @pl.when(i == pl.num_programs(1) - 1)
    def _():
        sre_ref[...] = sr
        sim_ref[...] = si


@functools.lru_cache(maxsize=None)
def _s5_unpermute(t):
    n = t // 8
    p = np.zeros((t, t), np.float32)
    for c in range(8):
        for j in range(n):
            p[c * n + j, j * 8 + c] = 1.0
    return p.astype(jnp.bfloat16)


def _s5_weight_specs(l):
    lay = lambda shape: pl.BlockSpec((None,) + shape, lambda *idx: (l, 0, 0))
    return [lay((GROUP_WIDTH, 2 * S5_STATE_WIDTH)), lay((2 * S5_POW, S5_STATE_WIDTH)),
            lay((S5_STATE_WIDTH, GROUP_WIDTH)), lay((S5_STATE_WIDTH, GROUP_WIDTH)),
            lay((1, GROUP_WIDTH)), lay((GROUP_WIDTH, GROUP_WIDTH)), lay((1, GROUP_WIDTH))]


def _s5_weights(W):
    return (W["s5_bbar"], W["s5_apow"], W["s5_c_re"], W["s5_c_im"], W["s5_d"], W["s5_glu_w"], W["s5_glu_b"])


def _s5_prompt(zs, l, W, bsz, seq):
    t = MIX_TILE
    nt = seq // t
    w = S5_STATE_WIDTH
    st = pl.BlockSpec((None, 1, w), lambda b, i: (b, 0, 0))
    return pl.pallas_call(
        _s5_prompt_kernel,
        grid=(bsz, nt),
        in_specs=[pl.BlockSpec((2 * t, 128), lambda b, i: (b * nt + i, 0))] + _s5_weight_specs(l)
        + [pl.BlockSpec((t, t), lambda b, i: (0, 0))],
        out_specs=[pl.BlockSpec((t, GROUP_WIDTH), lambda b, i: (b * nt + i, 0)), st, st],
        out_shape=[jax.ShapeDtypeStruct((bsz * seq, GROUP_WIDTH), BF16),
                   jax.ShapeDtypeStruct((bsz, 1, w), F32), jax.ShapeDtypeStruct((bsz, 1, w), F32)],
        scratch_shapes=[pltpu.VMEM((t, w), F32), pltpu.VMEM((t, w), F32),
                        pltpu.VMEM((1, w), F32), pltpu.VMEM((1, w), F32)],
        compiler_params=_params(("arbitrary", "arbitrary")),
        name="s5_prompt",
    )(zs.reshape(2 * bsz * seq, 128), *_s5_weights(W), _s5_unpermute(t))


def _s5_decode_kernel(u_ref, x0r_ref, x0i_ref, bbar_ref, apow_ref, cre_ref, cim_ref, d_ref, gw_ref, gb_ref,
                      o_ref, xr_ref, xi_ref):
    rows = u_ref.shape[0]
    w = S5_STATE_WIDTH
    u = u_ref[...]
    bu = _dot(u.astype(BF16), bbar_ref[...])
    first = (lax.broadcasted_iota(jnp.int32, (rows, 1), 0) & (DEC_LEN - 1)) == 0
    ar = apow_ref[0:1, :]
    ai = apow_ref[S5_POW:S5_POW + 1, :]
    x0r = x0r_ref[...]
    x0i = x0i_ref[...]
    br = bu[:, 0:w] + jnp.where(first, ar * x0r - ai * x0i, 0.0)
    bi = bu[:, w:] + jnp.where(first, ar * x0i + ai * x0r, 0.0)
    xr, xi = _s5_local_scan(br.reshape(rows // 8, 8, w), bi.reshape(rows // 8, 8, w), apow_ref, DEC_LEN)
    xr = xr.reshape(rows, w)
    xi = xi.reshape(rows, w)
    xr_ref[...] = xr
    xi_ref[...] = xi
    o_ref[...] = _s5_output(xr, xi, u, cre_ref, cim_ref, d_ref, gw_ref, gb_ref).astype(BF16)


def _s5_decode(zs, x0r, x0i, l, W):
    rows = zs.shape[0]
    rb = min(256, rows)
    w = S5_STATE_WIDTH
    blk = lambda width: pl.BlockSpec((rb, width), lambda i: (i, 0))
    return pl.pallas_call(
        _s5_decode_kernel,
        grid=(rows // rb,),
        in_specs=[blk(GROUP_WIDTH), blk(w), blk(w)] + _s5_weight_specs(l),
        out_specs=[blk(GROUP_WIDTH), blk(w), blk(w)],
        out_shape=[jax.ShapeDtypeStruct((rows, GROUP_WIDTH), BF16),
                   jax.ShapeDtypeStruct((rows, w), F32), jax.ShapeDtypeStruct((rows, w), F32)],
        compiler_params=_params(("arbitrary",)),
        name="s5_decode",
    )(zs, x0r, x0i, *_s5_weights(W))


def _swa_prompt_kernel(sink_ref, za_ref, zprev_ref, bias_ref, o_ref):
    i = pl.program_id(1)
    w = SWA_WINDOW
    nblk = za_ref.shape[0] // w
    kfull = jnp.concatenate([zprev_ref[:, 256:384], za_ref[:, 256:384]], axis=0).astype(BF16)
    vfull = jnp.concatenate([zprev_ref[:, 384:512], za_ref[:, 384:512]], axis=0).astype(BF16)
    lane_half = lax.broadcasted_iota(jnp.int32, (1, w), 1) >> 6
    col = lax.broadcasted_iota(jnp.int32, (1, 2 * w), 1)
    no_prev = (col < w) & (i == 0)
    for n in range(nblk):
        kk = kfull[n * w:(n + 2) * w]
        vv = vfull[n * w:(n + 2) * w]
        for j in range(2):
            qj = za_ref[n * w:(n + 1) * w, j * w:(j + 1) * w]
            on_j = lane_half == j
            oj = jnp.zeros((w, w), F32)
            for g in range(2):
                h = j * 2 + g
                qa = qj if g == j else pltpu.roll(qj, HEAD_DIM, axis=1)
                sc = _dot_nt(jnp.where(on_j, qa, 0.0).astype(BF16), kk) * K_SCALE + bias_ref[h]
                if n == 0:
                    sc = jnp.where(no_prev, MASK_VALUE, sc)
                sink = sink_ref[h]
                m = jnp.maximum(jnp.max(sc, axis=-1, keepdims=True), sink)
                p = jnp.exp(sc - m)
                den = jnp.sum(p, axis=-1, keepdims=True) + jnp.exp(sink - m)
                og = jnp.where(on_j, _dot(p.astype(BF16), vv) / den, 0.0)
                oj = oj + (og if g == j else pltpu.roll(og, HEAD_DIM, axis=1))
            o_ref[n * w:(n + 1) * w, j * w:(j + 1) * w] = oj.astype(BF16)


def _swa_prompt(za, sinks, bsz, seq):
    t = MIX_TILE
    nt = seq // t
    per = t // SWA_WINDOW
    nb = seq // SWA_WINDOW
    bias = _swa_prompt_bias()
    return pl.pallas_call(
        _swa_prompt_kernel,
        grid=(bsz, nt),
        in_specs=[pl.BlockSpec(memory_space=pltpu.SMEM),
                  pl.BlockSpec((t, 512), lambda b, i: (b * nt + i, 0)),
                  pl.BlockSpec((SWA_WINDOW, 512), lambda b, i: (b * nb + jnp.maximum(i * per - 1, 0), 0)),
                  pl.BlockSpec(bias.shape, lambda b, i: (0, 0, 0))],
        out_specs=pl.BlockSpec((t, GROUP_WIDTH), lambda b, i: (b * nt + i, 0)),
        out_shape=jax.ShapeDtypeStruct((bsz * seq, GROUP_WIDTH), BF16),
        compiler_params=_params(("arbitrary", "arbitrary")),
        name="swa_prompt",
    )(sinks, za, za, bias)


def _swa_decode_kernel(*refs, aliased):
    sink_ref, za_ref, kc_ref, vc_ref, bc_ref, bn_ref = refs[:6]
    o_ref, kn_ref, vn_ref, os_scr = refs[6 + (2 if aliased else 0):]
    rows = za_ref.shape[0]
    nseq = rows // DEC_LEN
    w = SWA_WINDOW
    keep = w - DEC_LEN
    lane_half = lax.broadcasted_iota(jnp.int32, (1, w), 1) >> 6
    sub8 = lax.broadcasted_iota(jnp.int32, (8, 1), 0)
    rhead = lax.broadcasted_iota(jnp.int32, (32, 1), 0) >> 3
    sink = jnp.zeros((32, 1), F32)
    for h in range(N_HEADS):
        sink = jnp.where(rhead == h, sink_ref[h], sink)
    os_scr[...] = jnp.zeros(os_scr.shape, F32)

    def body(s, carry):
        r8 = pl.multiple_of((s >> 1) * 8, 8)
        par = s & 1
        k8 = za_ref[pl.ds(r8, 8), 256:384]
        v8 = za_ref[pl.ds(r8, 8), 384:512]
        kn_ref[s, 0:keep, :] = kc_ref[s, DEC_LEN:w, :]
        vn_ref[s, 0:keep, :] = vc_ref[s, DEC_LEN:w, :]
        kn_ref[s, keep:w, :] = jnp.where(par == 0, k8[0:DEC_LEN, :], k8[DEC_LEN:8, :])
        vn_ref[s, keep:w, :] = jnp.where(par == 0, v8[0:DEC_LEN, :], v8[DEC_LEN:8, :])
        pieces = []
        for j in range(2):
            qj = za_ref[pl.ds(r8, 8), j * w:(j + 1) * w]
            for g in range(2):
                qa = qj if g == j else pltpu.roll(qj, HEAD_DIM, axis=1)
                pieces.append(jnp.where(lane_half == j, qa, 0.0))
        qs = jnp.concatenate(pieces, axis=0).astype(BF16)
        sc_c = _dot_nt(qs, kc_ref[s].astype(BF16)) * K_SCALE + bc_ref[par]
        sc_n = _dot_nt(qs, k8.astype(BF16)) * K_SCALE + bn_ref[par]
        m = jnp.maximum(jnp.maximum(jnp.max(sc_c, axis=-1, keepdims=True),
                                    jnp.max(sc_n, axis=-1, keepdims=True)), sink)
        p_c = jnp.exp(sc_c - m)
        p_n = jnp.exp(sc_n - m)
        den = jnp.sum(p_c, axis=-1, keepdims=True) + jnp.sum(p_n, axis=-1, keepdims=True) + jnp.exp(sink - m)
        o32 = (_dot(p_c.astype(BF16), vc_ref[s].astype(BF16)) + _dot(p_n.astype(BF16), v8.astype(BF16))) / den
        tiles = []
        for j in range(2):
            oj = jnp.zeros((8, w), F32)
            for g in range(2):
                h = j * 2 + g
                og = jnp.where(lane_half == j, o32[h * 8:(h + 1) * 8, :], 0.0)
                oj = oj + (og if g == j else pltpu.roll(og, HEAD_DIM, axis=1))
            tiles.append(oj)
        o8 = jnp.concatenate(tiles, axis=1)
        live8 = (sub8 >> 2) == par
        os_scr[pl.ds(r8, 8), :] = jnp.where(live8, o8, os_scr[pl.ds(r8, 8), :])
        return carry

    lax.fori_loop(0, nseq, body, 0)
    o_ref[...] = os_scr[...].astype(BF16)


def _swa_decode(za, kc, vc, prev_k, prev_v, l, sinks):
    rows = za.shape[0]
    rb = DEC_ROWS
    nseq = rb // DEC_LEN
    bc, bn = _swa_decode_bias()
    cache = pl.BlockSpec((None, nseq, SWA_WINDOW, 128), lambda i: (l, i, 0, 0))
    in_specs = [pl.BlockSpec(memory_space=pltpu.SMEM),
                pl.BlockSpec((rb, 512), lambda i: (i, 0)), cache, cache,
                pl.BlockSpec(bc.shape, lambda i: (0, 0, 0)), pl.BlockSpec(bn.shape, lambda i: (0, 0, 0))]
    args = [sinks, za, kc, vc, bc, bn]
    aliases = {}
    if prev_k is not None:
        aliases = {len(args): 1, len(args) + 1: 2}
        in_specs = in_specs + [pl.BlockSpec(memory_space=pl.ANY)] * 2
        args = args + [prev_k, prev_v]
    return pl.pallas_call(
        functools.partial(_swa_decode_kernel, aliased=prev_k is not None),
        grid=(rows // rb,),
        in_specs=in_specs,
        out_specs=[pl.BlockSpec((rb, GROUP_WIDTH), lambda i: (i, 0)), cache, cache],
        out_shape=[jax.ShapeDtypeStruct((rows, GROUP_WIDTH), BF16),
                   jax.ShapeDtypeStruct(kc.shape, F32), jax.ShapeDtypeStruct(vc.shape, F32)],
        input_output_aliases=aliases,
        scratch_shapes=[pltpu.VMEM((rb, GROUP_WIDTH), F32)],
        compiler_params=_params(("arbitrary",)),
        name="swa_decode",
    )(*args)


def _prepare_weights(w_in, w_out, norm_pre_mix, norm_post_mix, norm_pre_ffn, norm_post_ffn, ret_norm_w,
                     s5_a_re, s5_a_im, s5_log_step, s5_b_re, s5_b_im, s5_c_re, s5_c_im, s5_d, s5_glu_w,
                     s5_glu_b, hgrn_lower_bounds, hgrn_norm_w, swa_sinks, ffn_w_gate, ffn_w_up, ffn_w_down):
    eye = jnp.eye(16, dtype=F32)
    b_bd = lambda b: jnp.einsum("lgpc,gh->lgchp", b, eye).reshape(DEPTH, GROUP_WIDTH, S5_STATE_WIDTH)
    c_bd = lambda c: jnp.einsum("lgcp,gh->lgphc", c, eye).reshape(DEPTH, S5_STATE_WIDTH, GROUP_WIDTH)
    vec = lambda a, n: a.reshape(DEPTH, 1, n)
    bbar, apow, lb = _prep(vec(s5_a_re, S5_STATE_WIDTH), vec(s5_a_im, S5_STATE_WIDTH),
                           jnp.repeat(s5_log_step, 64, axis=1).reshape(DEPTH, 1, S5_STATE_WIDTH),
                           b_bd(s5_b_re), b_bd(s5_b_im), hgrn_lower_bounds)
    return dict(
        w_in=w_in.astype(BF16), w_out=w_out.astype(BF16),
        norm_pre_mix=vec(norm_pre_mix, D_MODEL), norm_post_mix=vec(norm_post_mix, D_MODEL),
        norm_pre_ffn=vec(norm_pre_ffn, D_MODEL), norm_post_ffn=vec(norm_post_ffn, D_MODEL),
        ret_norm_w=vec(ret_norm_w, GROUP_WIDTH), hgrn_norm_w=vec(hgrn_norm_w, GROUP_WIDTH), hgrn_lb=lb,
        s5_bbar=bbar, s5_apow=apow, s5_c_re=c_bd(s5_c_re).astype(BF16), s5_c_im=c_bd(s5_c_im).astype(BF16),
        s5_d=vec(s5_d, GROUP_WIDTH), s5_glu_w=s5_glu_w.astype(BF16), s5_glu_b=vec(s5_glu_b, GROUP_WIDTH),
        swa_sinks=swa_sinks,
        ffn_w_gate=ffn_w_gate.astype(BF16), ffn_w_up=ffn_w_up.astype(BF16), ffn_w_down=ffn_w_down.astype(BF16))


def _prompt_trunk(x, W):
    bsz, seq, _ = x.shape
    x = x.reshape(bsz * seq, D_MODEL)
    acc = [[] for _ in range(6)]
    for l in range(DEPTH):
        zr, zs, zh, za = _in_proj(x, l, W)
        o_ret, s_ret = _ret_prompt(zr, l, W, bsz, seq)
        o_s5, s_re, s_im = _s5_prompt(zs, l, W, bsz, seq)
        o_h, s_ht = _hgrn_prompt(zh, l, W, bsz, seq)
        o_a = _swa_prompt(za, W["swa_sinks"][l], bsz, seq)
        x = _out_ffn(x, (o_ret, o_s5, o_h, o_a), l, W)
        kv = za.reshape(bsz, seq, 512)[:, seq - SWA_WINDOW:, 256:]
        new = (s_ret, s_re.reshape(bsz, 16, 64), s_im.reshape(bsz, 16, 64), jnp.swapaxes(s_ht, -1, -2),
               kv[..., 0:128].reshape(bsz, SWA_WINDOW, 2, HEAD_DIM),
               kv[..., 128:].reshape(bsz, SWA_WINDOW, 2, HEAD_DIM))
        for a, n in zip(acc, new):
            a.append(n)
    return x.reshape(bsz, seq, D_MODEL), [jnp.stack(a) for a in acc]


def _decode_trunk(x, W, states):
    s_ret, s5_re, s5_im, s_hgrn, buf_k, buf_v = states
    bsz, seq, _ = x.shape
    w = S5_STATE_WIDTH
    x = x.reshape(bsz * seq, D_MODEL)
    st_ret = s_ret.reshape(DEPTH, bsz, GROUP_WIDTH, HEAD_DIM)
    st_hgrn = jnp.swapaxes(s_hgrn, -1, -2).reshape(DEPTH, bsz, GROUP_WIDTH, HEAD_DIM)
    kc = buf_k.reshape(DEPTH, bsz, SWA_WINDOW, 128)
    vc = buf_v.reshape(DEPTH, bsz, SWA_WINDOW, 128)
    n_ret = n_hgrn = n_k = n_v = None
    s5_new = [[], []]
    for l in range(DEPTH):
        zr, zs, zh, za = _in_proj(x, l, W)
        o_ret, n_ret = _rec_decode(zr, st_ret, n_ret, l, W, False)
        x0r = jnp.repeat(s5_re[l].reshape(bsz, w), DEC_LEN, axis=0)
        x0i = jnp.repeat(s5_im[l].reshape(bsz, w), DEC_LEN, axis=0)
        o_s5, xr, xi = _s5_decode(zs, x0r, x0i, l, W)
        o_h, n_hgrn = _rec_decode(zh, st_hgrn, n_hgrn, l, W, True)
        o_a, n_k, n_v = _swa_decode(za, kc, vc, n_k, n_v, l, W["swa_sinks"][l])
        x = _out_ffn(x, (o_ret, o_s5, o_h, o_a), l, W)
        for acc, xs in zip(s5_new, (xr, xi)):
            acc.append(xs.reshape(bsz, DEC_LEN, w)[:, DEC_LEN - 1].reshape(bsz, 16, 64))
    heads = (DEPTH, bsz, N_HEADS, HEAD_DIM, HEAD_DIM)
    new = [n_ret.reshape(heads), jnp.stack(s5_new[0]), jnp.stack(s5_new[1]),
           jnp.swapaxes(n_hgrn.reshape(heads), -1, -2),
           n_k.reshape(buf_k.shape), n_v.reshape(buf_v.shape)]
    return x.reshape(bsz, seq, D_MODEL), new


def kernel(x_prompt, x_sample, state_ret, state_s5_re, state_s5_im, state_hgrn, cache_swa_k, cache_swa_v,
           w_in, w_out, norm_pre_mix, norm_post_mix, norm_pre_ffn, norm_post_ffn, ret_norm_w,
           s5_a_re, s5_a_im, s5_log_step, s5_b_re, s5_b_im, s5_c_re, s5_c_im, s5_d, s5_glu_w, s5_glu_b,
           hgrn_lower_bounds, hgrn_norm_w, swa_sinks, ffn_w_gate, ffn_w_up, ffn_w_down):
    W = _prepare_weights(w_in, w_out, norm_pre_mix, norm_post_mix, norm_pre_ffn, norm_post_ffn, ret_norm_w,
                         s5_a_re, s5_a_im, s5_log_step, s5_b_re, s5_b_im, s5_c_re, s5_c_im, s5_d, s5_glu_w,
                         s5_glu_b, hgrn_lower_bounds, hgrn_norm_w, swa_sinks, ffn_w_gate, ffn_w_up, ffn_w_down)
    y_prompt, p_states = _prompt_trunk(x_prompt, W)
    y_sample, s_states = _decode_trunk(x_sample, W, (state_ret, state_s5_re, state_s5_im, state_hgrn,
                                                     cache_swa_k, cache_swa_v))
    return (y_prompt, y_sample, *p_states, *s_states)
```

```python
import functools
import math

import numpy as np
import jax
import jax.numpy as jnp
from jax import lax
from jax.experimental import pallas as pl
from jax.experimental.pallas import tpu as pltpu

F32 = jnp.float32
BF16 = jnp.bfloat16

D_MODEL = 1024
HEAD_DIM = 64
GROUP_WIDTH = 256
N_HEADS = 4
DEPTH = 4
SWA_WINDOW = 128
S5_STATE_WIDTH = 1024
FFN_HIDDEN = 2816
FFN_CHUNK = 256
IN_COLS = 2816
NORM_EPS = 1e-6
MASK_VALUE = -1e30
K_SCALE = HEAD_DIM ** -0.5
SUB = 4
VMEM_LIMIT = 56 * 1024 * 1024

ROW_TILE = 512
MIX_TILE = 256
DEC_ROWS = 128
DEC_LEN = 4
S5_POW = 32

_NT = (((1,), (1,)), ((), ()))
_TN = (((0,), (0,)), ((), ()))


def _dot(a, b):
    return jnp.dot(a, b, preferred_element_type=F32)


def _dot_nt(a, b):
    return lax.dot_general(a, b, _NT, preferred_element_type=F32)


def _dot_tn(a, b):
    return lax.dot_general(a, b, _TN, preferred_element_type=F32)


def _params(sem):
    return pltpu.CompilerParams(dimension_semantics=sem, vmem_limit_bytes=VMEM_LIMIT)


def _rms(x, w):
    return x * lax.rsqrt(jnp.mean(x * x, axis=-1, keepdims=True) + NORM_EPS) * w


def _sigmoid(x):
    return 1.0 / (1.0 + jnp.exp(-x))


def _split2(x):
    hi = x.astype(BF16)
    lo = (x - hi.astype(F32)).astype(BF16)
    return hi, lo


def _split3(x):
    hi = x.astype(BF16)
    r1 = x - hi.astype(F32)
    mid = r1.astype(BF16)
    lo = (r1 - mid.astype(F32)).astype(BF16)
    return hi, mid, lo


def _head_id(axis, shape):
    return lax.broadcasted_iota(jnp.int32, shape, axis) >> 6


def _block_diag(val, dtype):
    r = _head_id(0, (GROUP_WIDTH, GROUP_WIDTH))
    c = _head_id(1, (GROUP_WIDTH, GROUP_WIDTH))
    return jnp.where(r == c, val, 0.0).astype(dtype)


def _head_sum(x, ones_bd):
    hi, lo = _split2(x)
    return _dot(jnp.concatenate([hi, lo], axis=1), jnp.concatenate([ones_bd, ones_bd], axis=0))


def _head_norm_gate(o, g, w, center):
    mean_bd = _block_diag(1.0 / HEAD_DIM, BF16)
    if center:
        o = o - _head_sum(o, mean_bd)
    var = _head_sum(o * o, mean_bd)
    return o * lax.rsqrt(var + NORM_EPS) * w * (g * _sigmoid(g))


def _prep_kernel(are_ref, aim_ref, ls_ref, bre_ref, bim_ref, lbw_ref, bbar_ref, apow_ref, lb_ref):
    l = pl.program_id(0)
    a_re = are_ref[...]
    a_im = aim_ref[...]
    dt = jnp.exp(ls_ref[...])
    mag = jnp.exp(a_re * dt)
    ab_re = mag * jnp.cos(a_im * dt)
    ab_im = mag * jnp.sin(a_im * dt)
    den = a_re * a_re + a_im * a_im
    g_re = ((ab_re - 1.0) * a_re + ab_im * a_im) / den
    g_im = (ab_im * a_re - (ab_re - 1.0) * a_im) / den
    b_re = bre_ref[...]
    b_im = bim_ref[...]
    bbar_ref[:, 0:S5_STATE_WIDTH] = (g_re * b_re - g_im * b_im).astype(BF16)
    bbar_ref[:, S5_STATE_WIDTH:] = (g_re * b_im + g_im * b_re).astype(BF16)
    p_re, p_im = ab_re, ab_im
    for r in range(S5_POW):
        apow_ref[r:r + 1, :] = p_re
        apow_ref[S5_POW + r:S5_POW + r + 1, :] = p_im
        p_re, p_im = p_re * ab_re - p_im * ab_im, p_re * ab_im + p_im * ab_re
    w = lbw_ref[...]
    e = jnp.exp(w - jnp.max(w, axis=0, keepdims=True))
    sm = e / jnp.sum(e, axis=0, keepdims=True)
    row = lax.broadcasted_iota(jnp.int32, sm.shape, 0)
    lb_ref[...] = jnp.sum(jnp.where((row >= 1) & (row <= l), sm, 0.0), axis=0, keepdims=True)


def _prep(a_re, a_im, log_step, b_re_bd, b_im_bd, lower_bounds):
    vec = pl.BlockSpec((None, 1, S5_STATE_WIDTH), lambda l: (l, 0, 0))
    mat = pl.BlockSpec((None, GROUP_WIDTH, S5_STATE_WIDTH), lambda l: (l, 0, 0))
    return pl.pallas_call(
        _prep_kernel,
        grid=(DEPTH,),
        in_specs=[vec, vec, vec, mat, mat, pl.BlockSpec((DEPTH, GROUP_WIDTH), lambda l: (0, 0))],
        out_specs=[pl.BlockSpec((None, GROUP_WIDTH, 2 * S5_STATE_WIDTH), lambda l: (l, 0, 0)),
                   pl.BlockSpec((None, 2 * S5_POW, S5_STATE_WIDTH), lambda l: (l, 0, 0)),
                   pl.BlockSpec((None, 1, GROUP_WIDTH), lambda l: (l, 0, 0))],
        out_shape=[jax.ShapeDtypeStruct((DEPTH, GROUP_WIDTH, 2 * S5_STATE_WIDTH), BF16),
                   jax.ShapeDtypeStruct((DEPTH, 2 * S5_POW, S5_STATE_WIDTH), F32),
                   jax.ShapeDtypeStruct((DEPTH, 1, GROUP_WIDTH), F32)],
        compiler_params=_params(("arbitrary",)),
        name="prep_weights",
    )(a_re, a_im, log_step, b_re_bd, b_im_bd, lower_bounds)


def _in_kernel(x_ref, nw_ref, w_ref, zr_ref, zs0_ref, zs1_ref, zh_ref, za_ref):
    h = _rms(x_ref[...], nw_ref[...]).astype(BF16)
    zr_ref[...] = _dot(h, w_ref[:, 0:1024])
    zs0_ref[...] = _dot(h, w_ref[:, 1024:1152])
    zs1_ref[...] = _dot(h, w_ref[:, 1152:1280])
    zh_ref[...] = _dot(h, w_ref[:, 1280:2304])
    za_ref[...] = _dot(h, w_ref[:, 2304:2816])


def _in_proj(x, l, W):
    n = x.shape[0]
    tm = min(ROW_TILE, n)
    widths = (1024, 128, 128, 1024, 512)
    return pl.pallas_call(
        _in_kernel,
        grid=(n // tm,),
        in_specs=[pl.BlockSpec((tm, D_MODEL), lambda i: (i, 0)),
                  pl.BlockSpec((None, 1, D_MODEL), lambda i: (l, 0, 0)),
                  pl.BlockSpec((None, D_MODEL, IN_COLS), lambda i: (l, 0, 0))],
        out_specs=[pl.BlockSpec((tm, w), lambda i: (i, 0)) for w in widths],
        out_shape=[jax.ShapeDtypeStruct((n, w), F32) for w in widths],
        compiler_params=_params(("arbitrary",)),
        name="in_proj",
    )(x, W["norm_pre_mix"], W["w_in"])


def _out_kernel(x_ref, o1_ref, o2_ref, o3_ref, o4_ref, wo_ref, npm_ref, npf_ref, npo_ref,
                wg_ref, wu_ref, wd_ref, y_ref):
    m = _dot(o1_ref[...], wo_ref[0:256, :])
    m = m + _dot(o2_ref[...], wo_ref[256:512, :])
    m = m + _dot(o3_ref[...], wo_ref[512:768, :])
    m = m + _dot(o4_ref[...], wo_ref[768:1024, :])
    x1 = x_ref[...] + _rms(m, npm_ref[...])
    h = _rms(x1, npf_ref[...]).astype(BF16)
    acc = jnp.zeros(x1.shape, F32)
    for c in range(FFN_HIDDEN // FFN_CHUNK):
        cs = slice(c * FFN_CHUNK, (c + 1) * FFN_CHUNK)
        gate = _dot(h, wg_ref[:, cs])
        up = _dot(h, wu_ref[:, cs])
        f = (gate * _sigmoid(gate) * up).astype(BF16)
        acc = acc + _dot(f, wd_ref[cs, :])
    y_ref[...] = x1 + _rms(acc, npo_ref[...])


def _out_ffn(x, outs, l, W):
    n = x.shape[0]
    tm = min(ROW_TILE, n)
    row = lambda w: pl.BlockSpec((tm, w), lambda i: (i, 0))
    vec = pl.BlockSpec((None, 1, D_MODEL), lambda i: (l, 0, 0))
    once = dict(pipeline_mode=pl.Buffered(1))
    return pl.pallas_call(
        _out_kernel,
        grid=(n // tm,),
        in_specs=[row(D_MODEL)] + [row(GROUP_WIDTH)] * 4 + [
            pl.BlockSpec((None, D_MODEL, D_MODEL), lambda i: (l, 0, 0), **once),
            vec, vec, vec,
            pl.BlockSpec((None, D_MODEL, FFN_HIDDEN), lambda i: (l, 0, 0), **once),
            pl.BlockSpec((None, D_MODEL, FFN_HIDDEN), lambda i: (l, 0, 0), **once),
            pl.BlockSpec((None, FFN_HIDDEN, D_MODEL), lambda i: (l, 0, 0), **once)],
        out_specs=row(D_MODEL),
        out_shape=jax.ShapeDtypeStruct((n, D_MODEL), F32),
        compiler_params=_params(("arbitrary",)),
        name="out_ffn",
    )(x, *outs, W["w_out"], W["norm_post_mix"], W["norm_pre_ffn"], W["norm_post_ffn"],
      W["ffn_w_gate"], W["ffn_w_up"], W["ffn_w_down"])


def _ret_log_gamma():
    return np.log1p(-np.exp2(-5.0 - np.arange(N_HEADS, dtype=np.float64)))


def _lane_heads(v):
    return np.repeat(np.asarray(v, np.float64), HEAD_DIM)[None, :]


@functools.lru_cache(maxsize=None)
def _ret_prompt_consts(t):
    lg = _ret_log_gamma()
    idx = np.arange(t, dtype=np.float64)
    diff = idx[:, None] - idx[None, :]
    intra = np.where(diff >= 0, np.exp(np.maximum(diff, 0.0)[None] * lg[:, None, None]), 0.0)
    q_dec = np.exp((idx[:, None] + 1.0) * _lane_heads(lg))
    k_dec = np.exp((t - 1.0 - idx[:, None]) * _lane_heads(lg))
    c_dec = np.exp(t * _lane_heads(lg))
    return tuple(np.asarray(a, np.float32) for a in (intra, q_dec, k_dec, c_dec))


@functools.lru_cache(maxsize=None)
def _ret_decode_consts(rows):
    lg = _ret_log_gamma()
    t = (np.arange(rows) % DEC_LEN).astype(np.float64)
    g_pow = np.stack([np.exp(d * _lane_heads(lg)) for d in range(DEC_LEN)])
    q_dec = np.exp((t[:, None] + 1.0) * _lane_heads(lg))
    k_dec = np.exp((DEC_LEN - 1.0 - t[:, None]) * _lane_heads(lg))
    return tuple(np.asarray(a, np.float32) for a in (g_pow, q_dec, k_dec))


def _hgrn_levels(t):
    levels = []
    hs = t // 2
    while hs >= SUB:
        levels.append(hs)
        hs //= 2
    return tuple(levels)


@functools.lru_cache(maxsize=None)
def _hgrn_prompt_consts(t):
    i = np.arange(t)[:, None]
    j = np.arange(t)[None, :]
    tri = (j <= i).astype(np.float32).astype(jnp.bfloat16)
    half = t // 2
    ih, jh = i[:half], j[:, :half]
    masks = []
    for hs in _hgrn_levels(t)[1:]:
        same = (ih // (2 * hs)) == (jh // (2 * hs))
        masks.append(same & ((ih % (2 * hs)) >= hs) & ((jh % (2 * hs)) < hs))
    return tri, np.stack(masks).astype(np.float32)


def _alibi_slopes():
    return np.exp2(-8.0 * (np.arange(N_HEADS, dtype=np.float64) + 1.0) / N_HEADS)


@functools.lru_cache(maxsize=None)
def _swa_prompt_bias():
    w = SWA_WINDOW
    dist = (w + np.arange(w))[:, None] - np.arange(2 * w)[None, :]
    valid = (dist >= 0) & (dist < w)
    b = np.where(valid[None], -_alibi_slopes()[:, None, None] * dist[None], MASK_VALUE)
    return np.asarray(b, np.float32)


@functools.lru_cache(maxsize=None)
def _swa_decode_bias():
    w = SWA_WINDOW
    slopes = _alibi_slopes()
    bc = np.full((2, 32, w), MASK_VALUE, np.float64)
    bn = np.full((2, 32, 8), MASK_VALUE, np.float64)
    for par in range(2):
        for h in range(N_HEADS):
            for t8 in range(8):
                t = t8 - DEC_LEN * par
                if not 0 <= t < DEC_LEN:
                    continue
                r = h * 8 + t8
                dist = w + t - np.arange(w)
                bc[par, r] = np.where((dist >= 0) & (dist < w), -slopes[h] * dist, MASK_VALUE)
                for u in range(t + 1):
                    bn[par, r, u + DEC_LEN * par] = -slopes[h] * (t - u)
    return np.asarray(bc, np.float32), np.asarray(bn, np.float32)


def _ret_prompt_kernel(z_ref, intra_ref, qdec_ref, kdec_ref, cdec_ref, nw_ref, o_ref, sout_ref, s_scr):
    i = pl.program_id(1)

    @pl.when(i == 0)
    def _():
        s_scr[...] = jnp.zeros(s_scr.shape, F32)

    q = z_ref[:, 0:256]
    k = z_ref[:, 256:512] * K_SCALE
    v = z_ref[:, 512:768]
    g = z_ref[:, 768:1024]
    lane_head = _head_id(1, (1, GROUP_WIDTH))
    kb = k.astype(BF16)
    vb = v.astype(BF16)
    s_old = s_scr[...]
    o = _dot((q * qdec_ref[...]).astype(BF16), s_old.astype(BF16))
    for h in range(N_HEADS):
        mh = lane_head == h
        sc = _dot_nt(jnp.where(mh, q, 0.0).astype(BF16), kb)
        p = (sc * intra_ref[h]).astype(BF16)
        o = o + jnp.where(mh, _dot(p, vb), 0.0)
    upd = _dot_tn((k * kdec_ref[...]).astype(BF16), vb)
    s_new = cdec_ref[...] * s_old + upd * _block_diag(1.0, F32)
    s_scr[...] = s_new
    o_ref[...] = _head_norm_gate(o, g, nw_ref[...], True).astype(BF16)

    @pl.when(i == pl.num_programs(1) - 1)
    def _():
        for h in range(N_HEADS):
            hs = slice(h * HEAD_DIM, (h + 1) * HEAD_DIM)
            sout_ref[h] = s_new[hs, hs]


def _ret_prompt(zr, l, W, bsz, seq):
    t = MIX_TILE
    nt = seq // t
    intra, q_dec, k_dec, c_dec = _ret_prompt_consts(t)
    const = lambda shape: pl.BlockSpec(shape, lambda b, i: (0,) * len(shape))
    return pl.pallas_call(
        _ret_prompt_kernel,
        grid=(bsz, nt),
        in_specs=[pl.BlockSpec((t, 1024), lambda b, i: (b * nt + i, 0)),
                  const((N_HEADS, t, t)), const((t, GROUP_WIDTH)), const((t, GROUP_WIDTH)),
                  const((1, GROUP_WIDTH)),
                  pl.BlockSpec((None, 1, GROUP_WIDTH), lambda b, i: (l, 0, 0))],
        out_specs=[pl.BlockSpec((t, GROUP_WIDTH), lambda b, i: (b * nt + i, 0)),
                   pl.BlockSpec((None, N_HEADS, HEAD_DIM, HEAD_DIM), lambda b, i: (b, 0, 0, 0))],
        out_shape=[jax.ShapeDtypeStruct((bsz * seq, GROUP_WIDTH), BF16),
                   jax.ShapeDtypeStruct((bsz, N_HEADS, HEAD_DIM, HEAD_DIM), F32)],
        scratch_shapes=[pltpu.VMEM((GROUP_WIDTH, GROUP_WIDTH), F32)],
        compiler_params=_params(("arbitrary", "arbitrary")),
        name="ret_prompt",
    )(zr, intra, q_dec, k_dec, c_dec, W["ret_norm_w"])


def _hgrn_gates(hf, lb):
    en = jnp.exp(-hf)
    r = 1.0 / (1.0 + en)
    log_f = jnp.log(r) + jnp.log1p(lb * en)
    k = (1.0 - lb) * (en * r)
    return log_f, k


def _hgrn_prompt_kernel(z_ref, lb_ref, nw_ref, tri_ref, lmask_ref, o_ref, sout_ref, st_scr):
    i = pl.program_id(1)
    t = z_ref.shape[0]
    half = t // 2
    w = GROUP_WIDTH

    @pl.when(i == 0)
    def _():
        st_scr[...] = jnp.zeros(st_scr.shape, F32)

    q = z_ref[:, 0:256]
    v = z_ref[:, 512:768]
    g = z_ref[:, 768:1024]
    log_f, k = _hgrn_gates(z_ref[:, 256:512], lb_ref[...])
    lane_head = _head_id(1, (1, w))
    vb = v.astype(BF16)

    tri = tri_ref[...]
    b = _dot(jnp.concatenate([tri, tri, tri], axis=1), jnp.concatenate(_split3(log_f), axis=0))

    levels = _hgrn_levels(t)
    diag_sc = [[jnp.zeros((half, half), F32) for _ in range(N_HEADS)] for _ in range(2)]
    cross_sc = [None] * N_HEADS
    for li, hs in enumerate(levels):
        nb = t // (2 * hs)
        ref = jnp.broadcast_to(b.reshape(nb, 2 * hs, w)[:, hs - 1:hs, :], (nb, 2 * hs, w)).reshape(t, w)
        e = jnp.exp(-jnp.abs(b - ref))
        qe = (q * e).astype(BF16)
        ke = (k * e).astype(BF16)
        for h in range(N_HEADS):
            mh = lane_head == h
            if li == 0:
                cross_sc[h] = _dot_nt(jnp.where(mh, qe[half:], 0), ke[:half])
            else:
                lm = lmask_ref[li - 1]
                for hf in range(2):
                    rows = slice(hf * half, (hf + 1) * half)
                    diag_sc[hf][h] = diag_sc[hf][h] + lm * _dot_nt(jnp.where(mh, qe[rows], 0), ke[rows])

    ones_bd = _block_diag(1.0, BF16)
    g3 = (t // 8, 8, w)
    sub = lax.broadcasted_iota(jnp.int32, (1, 8, 1), 1) & (SUB - 1)
    q3, k3, v3, l3 = (a.reshape(g3) for a in (q, k, v, log_f))
    o3 = jnp.zeros(g3, F32)
    bd = jnp.zeros(g3, F32)
    for d in range(SUB):
        ks = k3 if d == 0 else pltpu.roll(k3, d, axis=1)
        vs = v3 if d == 0 else pltpu.roll(v3, d, axis=1)
        p = jnp.where(sub >= d, q3 * ks * jnp.exp(bd), 0.0)
        o3 = o3 + _dot(p.reshape(t, w).astype(BF16), ones_bd).reshape(g3) * vs
        if d + 1 < SUB:
            bd = bd + (l3 if d == 0 else pltpu.roll(l3, d, axis=1))
    o = o3.reshape(t, w)

    o_lo = jnp.zeros((half, w), F32)
    o_hi = jnp.zeros((half, w), F32)
    for h in range(N_HEADS):
        mh = lane_head == h
        o_lo = o_lo + jnp.where(mh, _dot(diag_sc[0][h].astype(BF16), vb[:half]), 0.0)
        p_hi = jnp.concatenate([cross_sc[h], diag_sc[1][h]], axis=1).astype(BF16)
        o_hi = o_hi + jnp.where(mh, _dot(p_hi, vb), 0.0)
    o = o + jnp.concatenate([o_lo, o_hi], axis=0)

    st_old = st_scr[...]
    b_last = b[t - 1:t, :]
    o = o + _dot_nt((q * jnp.exp(b)).astype(BF16), st_old.astype(BF16))
    upd = _dot_tn(vb, (k * jnp.exp(b_last - b)).astype(BF16))
    st_new = st_old * jnp.exp(b_last) + upd * _block_diag(1.0, F32)
    st_scr[...] = st_new
    o_ref[...] = _head_norm_gate(o, g, nw_ref[...], False).astype(BF16)

    @pl.when(i == pl.num_programs(1) - 1)
    def _():
        for h in range(N_HEADS):
            hs = slice(h * HEAD_DIM, (h + 1) * HEAD_DIM)
            sout_ref[h] = st_new[hs, hs]


def _hgrn_prompt(zh, l, W, bsz, seq):
    t = MIX_TILE
    nt = seq // t
    tri, lmask = _hgrn_prompt_consts(t)
    const = lambda shape: pl.BlockSpec(shape, lambda b, i: (0,) * len(shape))
    vec = pl.BlockSpec((None, 1, GROUP_WIDTH), lambda b, i: (l, 0, 0))
    return pl.pallas_call(
        _hgrn_prompt_kernel,
        grid=(bsz, nt),
        in_specs=[pl.BlockSpec((t, 1024), lambda b, i: (b * nt + i, 0)), vec, vec,
                  const(tri.shape), const(lmask.shape)],
        out_specs=[pl.BlockSpec((t, GROUP_WIDTH), lambda b, i: (b * nt + i, 0)),
                   pl.BlockSpec((None, N_HEADS, HEAD_DIM, HEAD_DIM), lambda b, i: (b, 0, 0, 0))],
        out_shape=[jax.ShapeDtypeStruct((bsz * seq, GROUP_WIDTH), BF16),
                   jax.ShapeDtypeStruct((bsz, N_HEADS, HEAD_DIM, HEAD_DIM), F32)],
        scratch_shapes=[pltpu.VMEM((GROUP_WIDTH, GROUP_WIDTH), F32)],
        compiler_params=_params(("arbitrary", "arbitrary")),
        name="hgrn_prompt",
    )(zh, W["hgrn_lb"], W["hgrn_norm_w"], tri, lmask)


def _rec_decode_kernel(*refs, hgrn, aliased):
    n_in = 4 if hgrn else 6
    if hgrn:
        z_ref, s_ref, lb_ref, nw_ref = refs[:n_in]
    else:
        z_ref, s_ref, gpow_ref, qdec_ref, kdec_ref, nw_ref = refs[:n_in]
    o_ref, sout_ref, qt_scr, os_scr, *more = refs[n_in + (1 if aliased else 0):]
    rows = z_ref.shape[0]
    nseq = rows // DEC_LEN
    w = GROUP_WIDTH
    q = z_ref[:, 0:256]
    v = z_ref[:, 512:768]
    g = z_ref[:, 768:1024]
    row = lax.broadcasted_iota(jnp.int32, (rows, 1), 0)
    if hgrn:
        log_f, k = _hgrn_gates(z_ref[:, 256:512], lb_ref[...])
    else:
        k = z_ref[:, 256:512] * K_SCALE

    ones_bd = _block_diag(1.0, BF16)
    g3 = (rows // 8, 8, w)
    tpos = lax.broadcasted_iota(jnp.int32, (1, 8, 1), 1) & (DEC_LEN - 1)
    q3, k3, v3 = (a.reshape(g3) for a in (q, k, v))
    o3 = jnp.zeros(g3, F32)
    if hgrn:
        l3 = log_f.reshape(g3)
        bd = jnp.zeros(g3, F32)
        b_rev = jnp.zeros(g3, F32)
    for d in range(DEC_LEN):
        valid = tpos >= d
        ks = k3 if d == 0 else pltpu.roll(k3, d, axis=1)
        vs = v3 if d == 0 else pltpu.roll(v3, d, axis=1)
        if hgrn:
            p = jnp.where(valid, q3 * ks * jnp.exp(bd), 0.0)
            o3 = o3 + _dot(p.reshape(rows, w).astype(BF16), ones_bd).reshape(g3) * vs
            bd = bd + jnp.where(valid, l3 if d == 0 else pltpu.roll(l3, d, axis=1), 0.0)
            if d > 0:
                b_rev = b_rev + jnp.where(tpos + d < DEC_LEN, pltpu.roll(l3, 8 - d, axis=1), 0.0)
        else:
            p = jnp.where(valid, q3 * ks, 0.0)
            o3 = o3 + _dot(p.reshape(rows, w).astype(BF16), ones_bd).reshape(g3) * gpow_ref[d] * vs
    o = o3.reshape(rows, w)
    if hgrn:
        dec_scr = more[0]
        b = bd.reshape(rows, w)
        b_rev = b_rev.reshape(rows, w)
        qt = q * jnp.exp(b)
        kt = k * jnp.exp(b_rev)
        dec_scr[...] = jnp.exp(b + b_rev)
        xmat, ymat = v, kt
    else:
        qt = q * qdec_ref[...]
        kt = k * kdec_ref[...]
        xmat, ymat = kt, v
    qt_scr[...] = qt
    r2 = lax.broadcasted_iota(jnp.int32, (w, w), 0)
    c2 = lax.broadcasted_iota(jnp.int32, (w, w), 1)
    eye = jnp.where(r2 == c2, 1.0, 0.0).astype(BF16)
    x_t = _dot_nt(eye, xmat.astype(BF16)).astype(BF16)
    y_heads = [ymat[:, h * HEAD_DIM:(h + 1) * HEAD_DIM] for h in range(N_HEADS)]
    sub8 = lax.broadcasted_iota(jnp.int32, (8, 1), 0)
    lg = _ret_log_gamma()

    def body(p, carry):
        r8 = pl.multiple_of(p * 8, 8)
        q8 = qt_scr[pl.ds(r8, 8), :]
        if hgrn:
            dec8 = dec_scr[pl.ds(r8, 8), :]
        o8 = None
        for par in range(2):
            s = 2 * p + par
            live = (row >> 2) == s
            outs = []
            for h in range(N_HEADS):
                hs = slice(h * HEAD_DIM, (h + 1) * HEAD_DIM)
                st = s_ref[s, hs, :]
                a8 = q8[:, hs].astype(BF16)
                if hgrn:
                    outs.append(_dot_nt(a8, st.astype(BF16)))
                    dec = dec8[par * DEC_LEN:par * DEC_LEN + 1, hs]
                else:
                    outs.append(_dot(a8, st.astype(BF16)))
                    dec = float(np.exp(DEC_LEN * lg[h]))
                ym = jnp.where(live, y_heads[h], 0.0).astype(BF16)
                sout_ref[s, hs, :] = dec * st + _dot(x_t[hs, :], ym)
            o_par = jnp.concatenate(outs, axis=1)
            o8 = o_par if par == 0 else jnp.where(sub8 < DEC_LEN, o8, o_par)
        os_scr[pl.ds(r8, 8), :] = o8
        return carry

    lax.fori_loop(0, nseq // 2, body, 0, unroll=4)
    o = o + os_scr[...]
    o_ref[...] = _head_norm_gate(o, g, nw_ref[...], not hgrn).astype(BF16)


def _rec_decode(z, states, prev, l, W, hgrn):
    rows = z.shape[0]
    rb = DEC_ROWS
    nseq = rb // DEC_LEN
    blk = lambda shape: pl.BlockSpec(shape, lambda i: (i,) + (0,) * (len(shape) - 1))
    const = lambda shape: pl.BlockSpec(shape, lambda i: (0,) * len(shape))
    vec = pl.BlockSpec((None, 1, GROUP_WIDTH), lambda i: (l, 0, 0))
    st_spec = pl.BlockSpec((None, nseq, GROUP_WIDTH, HEAD_DIM), lambda i: (l, i, 0, 0))
    scratch = [pltpu.VMEM((rb, GROUP_WIDTH), F32), pltpu.VMEM((rb, GROUP_WIDTH), F32)]
    if hgrn:
        in_specs = [blk((rb, 1024)), st_spec, vec, vec]
        args = [z, states, W["hgrn_lb"], W["hgrn_norm_w"]]
        scratch = scratch + [pltpu.VMEM((rb, GROUP_WIDTH), F32)]
    else:
        g_pow, q_dec, k_dec = _ret_decode_consts(rb)
        in_specs = [blk((rb, 1024)), st_spec, const(g_pow.shape), const(q_dec.shape), const(k_dec.shape), vec]
        args = [z, states, g_pow, q_dec, k_dec, W["ret_norm_w"]]
    aliases = {}
    if prev is not None:
        aliases = {len(args): 1}
        in_specs = in_specs + [pl.BlockSpec(memory_space=pl.ANY)]
        args = args + [prev]
    return pl.pallas_call(
        functools.partial(_rec_decode_kernel, hgrn=hgrn, aliased=prev is not None),
        grid=(rows // rb,),
        in_specs=in_specs,
        out_specs=[blk((rb, GROUP_WIDTH)), st_spec],
        out_shape=[jax.ShapeDtypeStruct((rows, GROUP_WIDTH), BF16),
                   jax.ShapeDtypeStruct(states.shape, F32)],
        input_output_aliases=aliases,
        scratch_shapes=scratch,
        compiler_params=_params(("arbitrary",)),
        name="hgrn_decode" if hgrn else "ret_decode",
    )(*args)


def _gelu_tanh(y):
    return 0.5 * y * (1.0 + jnp.tanh(math.sqrt(2.0 / math.pi) * (y + 0.044715 * (y * y * y))))


def _s5_output(xr, xi, u, cre_ref, cim_ref, d_ref, gw_ref, gb_ref):
    y = _dot(xr.astype(BF16), cre_ref[...]) - _dot(xi.astype(BF16), cim_ref[...]) + d_ref[...] * u
    y = _gelu_tanh(y)
    return y * _sigmoid(_dot(y.astype(BF16), gw_ref[...]) + gb_ref[...])


def _s5_local_scan(xr, xi, apow_ref, seg):
    sub = lax.broadcasted_iota(jnp.int32, (1, 8, 1), 1) & (seg - 1)
    d = 1
    while d < seg:
        ar = apow_ref[d - 1:d, :]
        ai = apow_ref[S5_POW + d - 1:S5_POW + d, :]
        keep = sub >= d
        sr = jnp.where(keep, pltpu.roll(xr, d, axis=1), 0.0)
        si = jnp.where(keep, pltpu.roll(xi, d, axis=1), 0.0)
        xr, xi = xr + (ar * sr - ai * si), xi + (ar * si + ai * sr)
        d *= 2
    return xr, xi


def _cmul(ar, ai, br, bi):
    return ar * br - ai * bi, ar * bi + ai * br


def _s5_prompt_kernel(u0_ref, u1_ref, bbar_ref, apow_ref, cre_ref, cim_ref, d_ref, gw_ref, gb_ref, perm_ref,
                      o_ref, sre_ref, sim_ref, xre_scr, xim_scr, cr_scr, ci_scr):
    i = pl.program_id(1)
    t = u0_ref.shape[0]
    w = S5_STATE_WIDTH
    n = S5_POW
    assert t == 8 * n

    @pl.when(i == 0)
    def _():
        cr_scr[...] = jnp.zeros(cr_scr.shape, F32)
        ci_scr[...] = jnp.zeros(ci_scr.shape, F32)

    u = jnp.concatenate(
        [jnp.concatenate([ref[pl.ds(j, 8, stride=n), :] for j in range(n)], axis=0) for ref in (u0_ref, u1_ref)],
        axis=1)
    bu = _dot(u.astype(BF16), bbar_ref[...])
    a_re = apow_ref[0:1, :]
    a_im = apow_ref[n:n + 1, :]

    xr = jnp.zeros((8, w), F32)
    xi = jnp.zeros((8, w), F32)
    for j in range(n):
        pr, pi = _cmul(a_re, a_im, xr, xi)
        xr = pr + bu[j * 8:(j + 1) * 8, 0:w]
        xi = pi + bu[j * 8:(j + 1) * 8, w:]
        xre_scr[j * 8:(j + 1) * 8, :] = xr
        xim_scr[j * 8:(j + 1) * 8, :] = xi

    an_re = apow_ref[n - 1:n, :]
    an_im = apow_ref[2 * n - 1:2 * n, :]
    sub = lax.broadcasted_iota(jnp.int32, (8, 1), 0)
    sr = cr_scr[...]
    si = ci_scr[...]
    in_re = jnp.zeros((8, w), F32)
    in_im = jnp.zeros((8, w), F32)
    for c in range(8):
        in_re = jnp.where(sub == c, sr, in_re)
        in_im = jnp.where(sub == c, si, in_im)
        pr, pi = _cmul(an_re, an_im, sr, si)
        sr = pr + xr[c:c + 1, :]
        si = pi + xi[c:c + 1, :]
    cr_scr[...] = sr
    ci_scr[...] = si

    for j in range(n):
        pr, pi = _cmul(apow_ref[j:j + 1, :], apow_ref[n + j:n + j + 1, :], in_re, in_im)
        xre_scr[j * 8:(j + 1) * 8, :] = xre_scr[j * 8:(j + 1) * 8, :] + pr
        xim_scr[j * 8:(j + 1) * 8, :] = xim_scr[j * 8:(j + 1) * 8, :] + pi

    y = _s5_output(xre_scr[...], xim_scr[...], u, cre_ref, cim_ref, d_ref, gw_ref, gb_ref)
    o_ref[...] = _dot(perm_ref[...], y.astype(BF16)).astype(BF16)

    @pl.when(i == pl.num_programs(1) - 1)
    def _():
        sre_ref[...] = sr
        sim_ref[...] = si


@functools.lru_cache(maxsize=None)
def _s5_unpermute(t):
    n = t // 8
    p = np.zeros((t, t), np.float32)
    for c in range(8):
        for j in range(n):
            p[c * n + j, j * 8 + c] = 1.0
    return p.astype(jnp.bfloat16)


def _s5_weight_specs(l):
    lay = lambda shape: pl.BlockSpec((None,) + shape, lambda *idx: (l, 0, 0))
    return [lay((GROUP_WIDTH, 2 * S5_STATE_WIDTH)), lay((2 * S5_POW, S5_STATE_WIDTH)),
            lay((S5_STATE_WIDTH, GROUP_WIDTH)), lay((S5_STATE_WIDTH, GROUP_WIDTH)),
            lay((1, GROUP_WIDTH)), lay((GROUP_WIDTH, GROUP_WIDTH)), lay((1, GROUP_WIDTH))]


def _s5_weights(W):
    return (W["s5_bbar"], W["s5_apow"], W["s5_c_re"], W["s5_c_im"], W["s5_d"], W["s5_glu_w"], W["s5_glu_b"])


def _s5_prompt(zs0, zs1, l, W, bsz, seq):
    t = MIX_TILE
    nt = seq // t
    w = S5_STATE_WIDTH
    st = pl.BlockSpec((None, 1, w), lambda b, i: (b, 0, 0))
    return pl.pallas_call(
        _s5_prompt_kernel,
        grid=(bsz, nt),
        in_specs=[pl.BlockSpec((t, 128), lambda b, i: (b * nt + i, 0))] * 2 + _s5_weight_specs(l)
        + [pl.BlockSpec((t, t), lambda b, i: (0, 0))],
        out_specs=[pl.BlockSpec((t, GROUP_WIDTH), lambda b, i: (b * nt + i, 0)), st, st],
        out_shape=[jax.ShapeDtypeStruct((bsz * seq, GROUP_WIDTH), BF16),
                   jax.ShapeDtypeStruct((bsz, 1, w), F32), jax.ShapeDtypeStruct((bsz, 1, w), F32)],
        scratch_shapes=[pltpu.VMEM((t, w), F32), pltpu.VMEM((t, w), F32),
                        pltpu.VMEM((1, w), F32), pltpu.VMEM((1, w), F32)],
        compiler_params=_params(("arbitrary", "arbitrary")),
        name="s5_prompt",
    )(zs0, zs1, *_s5_weights(W), _s5_unpermute(t))


def _s5_decode_kernel(u0_ref, u1_ref, x0r_ref, x0i_ref, bbar_ref, apow_ref, cre_ref, cim_ref, d_ref, gw_ref, gb_ref,
                      o_ref, xr_ref, xi_ref):
    rows = u0_ref.shape[0]
    w = S5_STATE_WIDTH
    u = jnp.concatenate([u0_ref[...], u1_ref[...]], axis=1)
    bu = _dot(u.astype(BF16), bbar_ref[...])
    first = (lax.broadcasted_iota(jnp.int32, (rows, 1), 0) & (DEC_LEN - 1)) == 0
    ar = apow_ref[0:1, :]
    ai = apow_ref[S5_POW:S5_POW + 1, :]
    x0r = x0r_ref[...]
    x0i = x0i_ref[...]
    br = bu[:, 0:w] + jnp.where(first, ar * x0r - ai * x0i, 0.0)
    bi = bu[:, w:] + jnp.where(first, ar * x0i + ai * x0r, 0.0)
    xr, xi = _s5_local_scan(br.reshape(rows // 8, 8, w), bi.reshape(rows // 8, 8, w), apow_ref, DEC_LEN)
    xr = xr.reshape(rows, w)
    xi = xi.reshape(rows, w)
    xr_ref[...] = xr
    xi_ref[...] = xi
    o_ref[...] = _s5_output(xr, xi, u, cre_ref, cim_ref, d_ref, gw_ref, gb_ref).astype(BF16)


def _s5_decode(zs0, zs1, x0r, x0i, l, W):
    rows = zs0.shape[0]
    rb = min(256, rows)
    w = S5_STATE_WIDTH
    blk = lambda width: pl.BlockSpec((rb, width), lambda i: (i, 0))
    return pl.pallas_call(
        _s5_decode_kernel,
        grid=(rows // rb,),
        in_specs=[blk(128), blk(128), blk(w), blk(w)] + _s5_weight_specs(l),
        out_specs=[blk(GROUP_WIDTH), blk(w), blk(w)],
        out_shape=[jax.ShapeDtypeStruct((rows, GROUP_WIDTH), BF16),
                   jax.ShapeDtypeStruct((rows, w), F32), jax.ShapeDtypeStruct((rows, w), F32)],
        compiler_params=_params(("arbitrary",)),
        name="s5_decode",
    )(zs0, zs1, x0r, x0i, *_s5_weights(W))


def _swa_prompt_kernel(sink_ref, za_ref, zprev_ref, bias_ref, o_ref):
    i = pl.program_id(1)
    w = SWA_WINDOW
    nblk = za_ref.shape[0] // w
    kfull = jnp.concatenate([zprev_ref[:, 256:384], za_ref[:, 256:384]], axis=0).astype(BF16)
    vfull = jnp.concatenate([zprev_ref[:, 384:512], za_ref[:, 384:512]], axis=0).astype(BF16)
    lane_half = lax.broadcasted_iota(jnp.int32, (1, w), 1) >> 6
    col = lax.broadcasted_iota(jnp.int32, (1, 2 * w), 1)
    no_prev = (col < w) & (i == 0)
    for n in range(nblk):
        kk = kfull[n * w:(n + 2) * w]
        vv = vfull[n * w:(n + 2) * w]
        for j in range(2):
            qj = za_ref[n * w:(n + 1) * w, j * w:(j + 1) * w]
            on_j = lane_half == j
            oj = jnp.zeros((w, w), F32)
            for g in range(2):
                h = j * 2 + g
                qa = qj if g == j else pltpu.roll(qj, HEAD_DIM, axis=1)
                sc = _dot_nt(jnp.where(on_j, qa, 0.0).astype(BF16), kk) * K_SCALE + bias_ref[h]
                if n == 0:
                    sc = jnp.where(no_prev, MASK_VALUE, sc)
                sink = sink_ref[h]
                m = jnp.maximum(jnp.max(sc, axis=-1, keepdims=True), sink)
                p = jnp.exp(sc - m)
                den = jnp.sum(p, axis=-1, keepdims=True) + jnp.exp(sink - m)
                og = jnp.where(on_j, _dot(p.astype(BF16), vv) / den, 0.0)
                oj = oj + (og if g == j else pltpu.roll(og, HEAD_DIM, axis=1))
            o_ref[n * w:(n + 1) * w, j * w:(j + 1) * w] = oj.astype(BF16)


def _swa_prompt(za, sinks, bsz, seq):
    t = MIX_TILE
    nt = seq // t
    per = t // SWA_WINDOW
    nb = seq // SWA_WINDOW
    bias = _swa_prompt_bias()
    return pl.pallas_call(
        _swa_prompt_kernel,
        grid=(bsz, nt),
        in_specs=[pl.BlockSpec(memory_space=pltpu.SMEM),
                  pl.BlockSpec((t, 512), lambda b, i: (b * nt + i, 0)),
                  pl.BlockSpec((SWA_WINDOW, 512), lambda b, i: (b * nb + jnp.maximum(i * per - 1, 0), 0)),
                  pl.BlockSpec(bias.shape, lambda b, i: (0, 0, 0))],
        out_specs=pl.BlockSpec((t, GROUP_WIDTH), lambda b, i: (b * nt + i, 0)),
        out_shape=jax.ShapeDtypeStruct((bsz * seq, GROUP_WIDTH), BF16),
        compiler_params=_params(("arbitrary", "arbitrary")),
        name="swa_prompt",
    )(sinks, za, za, bias)


def _swa_decode_kernel(*refs, aliased):
    sink_ref, za_ref, kc_ref, vc_ref, bc_ref, bn_ref = refs[:6]
    o_ref, kn_ref, vn_ref, os_scr = refs[6 + (2 if aliased else 0):]
    rows = za_ref.shape[0]
    nseq = rows // DEC_LEN
    w = SWA_WINDOW
    keep = w - DEC_LEN
    lane_half = lax.broadcasted_iota(jnp.int32, (1, w), 1) >> 6
    sub8 = lax.broadcasted_iota(jnp.int32, (8, 1), 0)
    rhead = lax.broadcasted_iota(jnp.int32, (32, 1), 0) >> 3
    sink = jnp.zeros((32, 1), F32)
    for h in range(N_HEADS):
        sink = jnp.where(rhead == h, sink_ref[h], sink)

    def body(p, carry):
        r8 = pl.multiple_of(p * 8, 8)
        k8 = za_ref[pl.ds(r8, 8), 256:384]
        v8 = za_ref[pl.ds(r8, 8), 384:512]
        pieces = []
        for j in range(2):
            qj = za_ref[pl.ds(r8, 8), j * w:(j + 1) * w]
            for g in range(2):
                qa = qj if g == j else pltpu.roll(qj, HEAD_DIM, axis=1)
                pieces.append(jnp.where(lane_half == j, qa, 0.0))
        qs = jnp.concatenate(pieces, axis=0).astype(BF16)
        o8 = None
        for par in range(2):
            o_par = one_sequence(2 * p + par, par, k8, v8, qs)
            o8 = o_par if par == 0 else jnp.where(sub8 < DEC_LEN, o8, o_par)
        os_scr[pl.ds(r8, 8), :] = o8
        return carry

    def one_sequence(s, par, k8, v8, qs):
        kn_ref[s, 0:keep, :] = kc_ref[s, DEC_LEN:w, :]
        vn_ref[s, 0:keep, :] = vc_ref[s, DEC_LEN:w, :]
        kn_ref[s, keep:w, :] = k8[par * DEC_LEN:(par + 1) * DEC_LEN, :]
        vn_ref[s, keep:w, :] = v8[par * DEC_LEN:(par + 1) * DEC_LEN, :]
        sc_c = _dot_nt(qs, kc_ref[s].astype(BF16)) * K_SCALE + bc_ref[par]
        sc_n = _dot_nt(qs, k8.astype(BF16)) * K_SCALE + bn_ref[par]
        m = jnp.maximum(jnp.maximum(jnp.max(sc_c, axis=-1, keepdims=True),
                                    jnp.max(sc_n, axis=-1, keepdims=True)), sink)
        p_c = jnp.exp(sc_c - m)
        p_n = jnp.exp(sc_n - m)
        den = jnp.sum(p_c, axis=-1, keepdims=True) + jnp.sum(p_n, axis=-1, keepdims=True) + jnp.exp(sink - m)
        o32 = (_dot(p_c.astype(BF16), vc_ref[s].astype(BF16)) + _dot(p_n.astype(BF16), v8.astype(BF16))) / den
        tiles = []
        for j in range(2):
            oj = jnp.zeros((8, w), F32)
            for g in range(2):
                h = j * 2 + g
                og = jnp.where(lane_half == j, o32[h * 8:(h + 1) * 8, :], 0.0)
                oj = oj + (og if g == j else pltpu.roll(og, HEAD_DIM, axis=1))
            tiles.append(oj)
        return jnp.concatenate(tiles, axis=1)

    lax.fori_loop(0, nseq // 2, body, 0, unroll=8)
    o_ref[...] = os_scr[...].astype(BF16)


def _swa_decode(za, kc, vc, prev_k, prev_v, l, sinks):
    rows = za.shape[0]
    rb = DEC_ROWS
    nseq = rb // DEC_LEN
    bc, bn = _swa_decode_bias()
    cache = pl.BlockSpec((None, nseq, SWA_WINDOW, 128), lambda i: (l, i, 0, 0))
    in_specs = [pl.BlockSpec(memory_space=pltpu.SMEM),
                pl.BlockSpec((rb, 512), lambda i: (i, 0)), cache, cache,
                pl.BlockSpec(bc.shape, lambda i: (0, 0, 0)), pl.BlockSpec(bn.shape, lambda i: (0, 0, 0))]
    args = [sinks, za, kc, vc, bc, bn]
    aliases = {}
    if prev_k is not None:
        aliases = {len(args): 1, len(args) + 1: 2}
        in_specs = in_specs + [pl.BlockSpec(memory_space=pl.ANY)] * 2
        args = args + [prev_k, prev_v]
    return pl.pallas_call(
        functools.partial(_swa_decode_kernel, aliased=prev_k is not None),
        grid=(rows // rb,),
        in_specs=in_specs,
        out_specs=[pl.BlockSpec((rb, GROUP_WIDTH), lambda i: (i, 0)), cache, cache],
        out_shape=[jax.ShapeDtypeStruct((rows, GROUP_WIDTH), BF16),
                   jax.ShapeDtypeStruct(kc.shape, F32), jax.ShapeDtypeStruct(vc.shape, F32)],
        input_output_aliases=aliases,
        scratch_shapes=[pltpu.VMEM((rb, GROUP_WIDTH), F32)],
        compiler_params=_params(("arbitrary",)),
        name="swa_decode",
    )(*args)


def _prepare_weights(w_in, w_out, norm_pre_mix, norm_post_mix, norm_pre_ffn, norm_post_ffn, ret_norm_w,
                     s5_a_re, s5_a_im, s5_log_step, s5_b_re, s5_b_im, s5_c_re, s5_c_im, s5_d, s5_glu_w,
                     s5_glu_b, hgrn_lower_bounds, hgrn_norm_w, swa_sinks, ffn_w_gate, ffn_w_up, ffn_w_down):
    eye = jnp.eye(16, dtype=F32)
    b_bd = lambda b: jnp.einsum("lgpc,gh->lgchp", b, eye).reshape(DEPTH, GROUP_WIDTH, S5_STATE_WIDTH)
    c_bd = lambda c: jnp.einsum("lgcp,gh->lgphc", c, eye).reshape(DEPTH, S5_STATE_WIDTH, GROUP_WIDTH)
    vec = lambda a, n: a.reshape(DEPTH, 1, n)
    bbar, apow, lb = _prep(vec(s5_a_re, S5_STATE_WIDTH), vec(s5_a_im, S5_STATE_WIDTH),
                           jnp.repeat(s5_log_step, 64, axis=1).reshape(DEPTH, 1, S5_STATE_WIDTH),
                           b_bd(s5_b_re), b_bd(s5_b_im), hgrn_lower_bounds)
    return dict(
        w_in=w_in.astype(BF16), w_out=w_out.astype(BF16),
        norm_pre_mix=vec(norm_pre_mix, D_MODEL), norm_post_mix=vec(norm_post_mix, D_MODEL),
        norm_pre_ffn=vec(norm_pre_ffn, D_MODEL), norm_post_ffn=vec(norm_post_ffn, D_MODEL),
        ret_norm_w=vec(ret_norm_w, GROUP_WIDTH), hgrn_norm_w=vec(hgrn_norm_w, GROUP_WIDTH), hgrn_lb=lb,
        s5_bbar=bbar, s5_apow=apow, s5_c_re=c_bd(s5_c_re).astype(BF16), s5_c_im=c_bd(s5_c_im).astype(BF16),
        s5_d=vec(s5_d, GROUP_WIDTH), s5_glu_w=s5_glu_w.astype(BF16), s5_glu_b=vec(s5_glu_b, GROUP_WIDTH),
        swa_sinks=swa_sinks,
        ffn_w_gate=ffn_w_gate.astype(BF16), ffn_w_up=ffn_w_up.astype(BF16), ffn_w_down=ffn_w_down.astype(BF16))


def _prompt_trunk(x, W):
    bsz, seq, _ = x.shape
    x = x.reshape(bsz * seq, D_MODEL)
    acc = [[] for _ in range(6)]
    for l in range(DEPTH):
        zr, zs0, zs1, zh, za = _in_proj(x, l, W)
        o_ret, s_ret = _ret_prompt(zr, l, W, bsz, seq)
        o_s5, s_re, s_im = _s5_prompt(zs0, zs1, l, W, bsz, seq)
        o_h, s_ht = _hgrn_prompt(zh, l, W, bsz, seq)
        o_a = _swa_prompt(za, W["swa_sinks"][l], bsz, seq)
        x = _out_ffn(x, (o_ret, o_s5, o_h, o_a), l, W)
        kv = za.reshape(bsz, seq, 512)[:, seq - SWA_WINDOW:, 256:]
        new = (s_ret, s_re.reshape(bsz, 16, 64), s_im.reshape(bsz, 16, 64), jnp.swapaxes(s_ht, -1, -2),
               kv[..., 0:128].reshape(bsz, SWA_WINDOW, 2, HEAD_DIM),
               kv[..., 128:].reshape(bsz, SWA_WINDOW, 2, HEAD_DIM))
        for a, n in zip(acc, new):
            a.append(n)
    return x.reshape(bsz, seq, D_MODEL), [jnp.stack(a) for a in acc]


def _decode_trunk(x, W, states):
    s_ret, s5_re, s5_im, s_hgrn, buf_k, buf_v = states
    bsz, seq, _ = x.shape
    w = S5_STATE_WIDTH
    x = x.reshape(bsz * seq, D_MODEL)
    st_ret = s_ret.reshape(DEPTH, bsz, GROUP_WIDTH, HEAD_DIM)
    st_hgrn = jnp.swapaxes(s_hgrn, -1, -2).reshape(DEPTH, bsz, GROUP_WIDTH, HEAD_DIM)
    kc = buf_k.reshape(DEPTH, bsz, SWA_WINDOW, 128)
    vc = buf_v.reshape(DEPTH, bsz, SWA_WINDOW, 128)
    n_ret = n_hgrn = n_k = n_v = None
    s5_new = [[], []]
    for l in range(DEPTH):
        zr, zs0, zs1, zh, za = _in_proj(x, l, W)
        o_ret, n_ret = _rec_decode(zr, st_ret, n_ret, l, W, False)
        x0r = jnp.repeat(s5_re[l].reshape(bsz, w), DEC_LEN, axis=0)
        x0i = jnp.repeat(s5_im[l].reshape(bsz, w), DEC_LEN, axis=0)
        o_s5, xr, xi = _s5_decode(zs0, zs1, x0r, x0i, l, W)
        o_h, n_hgrn = _rec_decode(zh, st_hgrn, n_hgrn, l, W, True)
        o_a, n_k, n_v = _swa_decode(za, kc, vc, n_k, n_v, l, W["swa_sinks"][l])
        x = _out_ffn(x, (o_ret, o_s5, o_h, o_a), l, W)
        for acc, xs in zip(s5_new, (xr, xi)):
            acc.append(xs.reshape(bsz, DEC_LEN, w)[:, DEC_LEN - 1].reshape(bsz, 16, 64))
    heads = (DEPTH, bsz, N_HEADS, HEAD_DIM, HEAD_DIM)
    new = [n_ret.reshape(heads), jnp.stack(s5_new[0]), jnp.stack(s5_new[1]),
           jnp.swapaxes(n_hgrn.reshape(heads), -1, -2),
           n_k.reshape(buf_k.shape), n_v.reshape(buf_v.shape)]
    return x.reshape(bsz, seq, D_MODEL), new


def kernel(x_prompt, x_sample, state_ret, state_s5_re, state_s5_im, state_hgrn, cache_swa_k, cache_swa_v,
           w_in, w_out, norm_pre_mix, norm_post_mix, norm_pre_ffn, norm_post_ffn, ret_norm_w,
           s5_a_re, s5_a_im, s5_log_step, s5_b_re, s5_b_im, s5_c_re, s5_c_im, s5_d, s5_glu_w, s5_glu_b,
           hgrn_lower_bounds, hgrn_norm_w, swa_sinks, ffn_w_gate, ffn_w_up, ffn_w_down):
    W = _prepare_weights(w_in, w_out, norm_pre_mix, norm_post_mix, norm_pre_ffn, norm_post_ffn, ret_norm_w,
                         s5_a_re, s5_a_im, s5_log_step, s5_b_re, s5_b_im, s5_c_re, s5_c_im, s5_d, s5_glu_w,
                         s5_glu_b, hgrn_lower_bounds, hgrn_norm_w, swa_sinks, ffn_w_gate, ffn_w_up, ffn_w_down)
    y_prompt, p_states = _prompt_trunk(x_prompt, W)
    y_sample, s_states = _decode_trunk(x_sample, W, (state_ret, state_s5_re, state_s5_im, state_hgrn,
                                                     cache_swa_k, cache_swa_v))
    return (y_prompt, y_sample, *p_states, *s_states)
```

```python
import functools
import math

import numpy as np
import jax
import jax.numpy as jnp
from jax import lax
from jax.experimental import pallas as pl
from jax.experimental.pallas import tpu as pltpu

F32 = jnp.float32
BF16 = jnp.bfloat16

D_MODEL = 1024
HEAD_DIM = 64
GROUP_WIDTH = 256
N_HEADS = 4
DEPTH = 4
SWA_WINDOW = 128
S5_STATE_WIDTH = 1024
FFN_HIDDEN = 2816
FFN_CHUNK = 256
IN_COLS = 2816
NORM_EPS = 1e-6
MASK_VALUE = -1e30
K_SCALE = HEAD_DIM ** -0.5
SUB = 4
VMEM_LIMIT = 56 * 1024 * 1024

ROW_TILE = 512
MIX_TILE = 256
DEC_ROWS = 128
DEC_LEN = 4
S5_POW = 32

_NT = (((1,), (1,)), ((), ()))
_TN = (((0,), (0,)), ((), ()))


def _dot(a, b):
    return jnp.dot(a, b, preferred_element_type=F32)


def _dot_nt(a, b):
    return lax.dot_general(a, b, _NT, preferred_element_type=F32)


def _dot_tn(a, b):
    return lax.dot_general(a, b, _TN, preferred_element_type=F32)


def _params(sem):
    return pltpu.CompilerParams(dimension_semantics=sem, vmem_limit_bytes=VMEM_LIMIT)


def _rms(x, w):
    return x * lax.rsqrt(jnp.mean(x * x, axis=-1, keepdims=True) + NORM_EPS) * w


def _sigmoid(x):
    return 1.0 / (1.0 + jnp.exp(-x))


def _split2(x):
    hi = x.astype(BF16)
    lo = (x - hi.astype(F32)).astype(BF16)
    return hi, lo


def _split3(x):
    hi = x.astype(BF16)
    r1 = x - hi.astype(F32)
    mid = r1.astype(BF16)
    lo = (r1 - mid.astype(F32)).astype(BF16)
    return hi, mid, lo


def _head_id(axis, shape):
    return lax.broadcasted_iota(jnp.int32, shape, axis) >> 6


def _block_diag(val, dtype):
    r = _head_id(0, (GROUP_WIDTH, GROUP_WIDTH))
    c = _head_id(1, (GROUP_WIDTH, GROUP_WIDTH))
    return jnp.where(r == c, val, 0.0).astype(dtype)


def _head_sum(x, ones_bd):
    hi, lo = _split2(x)
    return _dot(jnp.concatenate([hi, lo], axis=1), jnp.concatenate([ones_bd, ones_bd], axis=0))


def _head_norm_gate(o, g, w, center):
    mean_bd = _block_diag(1.0 / HEAD_DIM, BF16)
    if center:
        o = o - _head_sum(o, mean_bd)
    var = _head_sum(o * o, mean_bd)
    return o * lax.rsqrt(var + NORM_EPS) * w * (g * _sigmoid(g))


def _prep_kernel(are_ref, aim_ref, ls_ref, bre_ref, bim_ref, lbw_ref, bbar_ref, apow_ref, lb_ref):
    l = pl.program_id(0)
    a_re = are_ref[...]
    a_im = aim_ref[...]
    dt = jnp.exp(ls_ref[...])
    mag = jnp.exp(a_re * dt)
    ab_re = mag * jnp.cos(a_im * dt)
    ab_im = mag * jnp.sin(a_im * dt)
    den = a_re * a_re + a_im * a_im
    g_re = ((ab_re - 1.0) * a_re + ab_im * a_im) / den
    g_im = (ab_im * a_re - (ab_re - 1.0) * a_im) / den
    b_re = bre_ref[...]
    b_im = bim_ref[...]
    bbar_ref[:, 0:S5_STATE_WIDTH] = (g_re * b_re - g_im * b_im).astype(BF16)
    bbar_ref[:, S5_STATE_WIDTH:] = (g_re * b_im + g_im * b_re).astype(BF16)
    p_re, p_im = ab_re, ab_im
    for r in range(S5_POW):
        apow_ref[r:r + 1, :] = p_re
        apow_ref[S5_POW + r:S5_POW + r + 1, :] = p_im
        p_re, p_im = p_re * ab_re - p_im * ab_im, p_re * ab_im + p_im * ab_re
    w = lbw_ref[...]
    e = jnp.exp(w - jnp.max(w, axis=0, keepdims=True))
    sm = e / jnp.sum(e, axis=0, keepdims=True)
    row = lax.broadcasted_iota(jnp.int32, sm.shape, 0)
    lb_ref[...] = jnp.sum(jnp.where((row >= 1) & (row <= l), sm, 0.0), axis=0, keepdims=True)


def _prep(a_re, a_im, log_step, b_re_bd, b_im_bd, lower_bounds):
    vec = pl.BlockSpec((None, 1, S5_STATE_WIDTH), lambda l: (l, 0, 0))
    mat = pl.BlockSpec((None, GROUP_WIDTH, S5_STATE_WIDTH), lambda l: (l, 0, 0))
    return pl.pallas_call(
        _prep_kernel,
        grid=(DEPTH,),
        in_specs=[vec, vec, vec, mat, mat, pl.BlockSpec((DEPTH, GROUP_WIDTH), lambda l: (0, 0))],
        out_specs=[pl.BlockSpec((None, GROUP_WIDTH, 2 * S5_STATE_WIDTH), lambda l: (l, 0, 0)),
                   pl.BlockSpec((None, 2 * S5_POW, S5_STATE_WIDTH), lambda l: (l, 0, 0)),
                   pl.BlockSpec((None, 1, GROUP_WIDTH), lambda l: (l, 0, 0))],
        out_shape=[jax.ShapeDtypeStruct((DEPTH, GROUP_WIDTH, 2 * S5_STATE_WIDTH), BF16),
                   jax.ShapeDtypeStruct((DEPTH, 2 * S5_POW, S5_STATE_WIDTH), F32),
                   jax.ShapeDtypeStruct((DEPTH, 1, GROUP_WIDTH), F32)],
        compiler_params=_params(("arbitrary",)),
        name="prep_weights",
    )(a_re, a_im, log_step, b_re_bd, b_im_bd, lower_bounds)


def _in_kernel(x_ref, nw_ref, w_ref, zr_ref, zs0_ref, zs1_ref, zh_ref, za_ref):
    h = _rms(x_ref[...], nw_ref[...]).astype(BF16)
    zr_ref[...] = _dot(h, w_ref[:, 0:1024])
    zs0_ref[...] = _dot(h, w_ref[:, 1024:1152])
    zs1_ref[...] = _dot(h, w_ref[:, 1152:1280])
    zh_ref[...] = _dot(h, w_ref[:, 1280:2304])
    za_ref[...] = _dot(h, w_ref[:, 2304:2816])


def _in_proj(x, l, W):
    n = x.shape[0]
    tm = min(ROW_TILE, n)
    widths = (1024, 128, 128, 1024, 512)
    return pl.pallas_call(
        _in_kernel,
        grid=(n // tm,),
        in_specs=[pl.BlockSpec((tm, D_MODEL), lambda i: (i, 0)),
                  pl.BlockSpec((None, 1, D_MODEL), lambda i: (l, 0, 0)),
                  pl.BlockSpec((None, D_MODEL, IN_COLS), lambda i: (l, 0, 0))],
        out_specs=[pl.BlockSpec((tm, w), lambda i: (i, 0)) for w in widths],
        out_shape=[jax.ShapeDtypeStruct((n, w), F32) for w in widths],
        compiler_params=_params(("arbitrary",)),
        name="in_proj",
    )(x, W["norm_pre_mix"], W["w_in"])


def _out_kernel(x_ref, o1_ref, o2_ref, o3_ref, o4_ref, wo_ref, npm_ref, npf_ref, npo_ref,
                wg_ref, wu_ref, wd_ref, y_ref):
    m = _dot(o1_ref[...], wo_ref[0:256, :])
    m = m + _dot(o2_ref[...], wo_ref[256:512, :])
    m = m + _dot(o3_ref[...], wo_ref[512:768, :])
    m = m + _dot(o4_ref[...], wo_ref[768:1024, :])
    x1 = x_ref[...] + _rms(m, npm_ref[...])
    h = _rms(x1, npf_ref[...]).astype(BF16)
    acc = jnp.zeros(x1.shape, F32)
    for c in range(FFN_HIDDEN // FFN_CHUNK):
        cs = slice(c * FFN_CHUNK, (c + 1) * FFN_CHUNK)
        gate = _dot(h, wg_ref[:, cs])
        up = _dot(h, wu_ref[:, cs])
        f = (gate * _sigmoid(gate) * up).astype(BF16)
        acc = acc + _dot(f, wd_ref[cs, :])
    y_ref[...] = x1 + _rms(acc, npo_ref[...])


def _out_ffn(x, outs, l, W):
    n = x.shape[0]
    tm = min(ROW_TILE, n)
    row = lambda w: pl.BlockSpec((tm, w), lambda i: (i, 0))
    vec = pl.BlockSpec((None, 1, D_MODEL), lambda i: (l, 0, 0))
    once = dict(pipeline_mode=pl.Buffered(1))
    return pl.pallas_call(
        _out_kernel,
        grid=(n // tm,),
        in_specs=[row(D_MODEL)] + [row(GROUP_WIDTH)] * 4 + [
            pl.BlockSpec((None, D_MODEL, D_MODEL), lambda i: (l, 0, 0), **once),
            vec, vec, vec,
            pl.BlockSpec((None, D_MODEL, FFN_HIDDEN), lambda i: (l, 0, 0), **once),
            pl.BlockSpec((None, D_MODEL, FFN_HIDDEN), lambda i: (l, 0, 0), **once),
            pl.BlockSpec((None, FFN_HIDDEN, D_MODEL), lambda i: (l, 0, 0), **once)],
        out_specs=row(D_MODEL),
        out_shape=jax.ShapeDtypeStruct((n, D_MODEL), F32),
        compiler_params=_params(("arbitrary",)),
        name="out_ffn",
    )(x, *outs, W["w_out"], W["norm_post_mix"], W["norm_pre_ffn"], W["norm_post_ffn"],
      W["ffn_w_gate"], W["ffn_w_up"], W["ffn_w_down"])


def _ret_log_gamma():
    return np.log1p(-np.exp2(-5.0 - np.arange(N_HEADS, dtype=np.float64)))


def _lane_heads(v):
    return np.repeat(np.asarray(v, np.float64), HEAD_DIM)[None, :]


@functools.lru_cache(maxsize=None)
def _ret_prompt_consts(t):
    lg = _ret_log_gamma()
    idx = np.arange(t, dtype=np.float64)
    diff = idx[:, None] - idx[None, :]
    intra = np.where(diff >= 0, np.exp(np.maximum(diff, 0.0)[None] * lg[:, None, None]), 0.0)
    q_dec = np.exp((idx[:, None] + 1.0) * _lane_heads(lg))
    k_dec = np.exp((t - 1.0 - idx[:, None]) * _lane_heads(lg))
    c_dec = np.exp(t * _lane_heads(lg))
    return tuple(np.asarray(a, np.float32) for a in (intra, q_dec, k_dec, c_dec))


@functools.lru_cache(maxsize=None)
def _ret_decode_consts(rows):
    lg = _ret_log_gamma()
    t = (np.arange(rows) % DEC_LEN).astype(np.float64)
    g_pow = np.stack([np.exp(d * _lane_heads(lg)) for d in range(DEC_LEN)])
    q_dec = np.exp((t[:, None] + 1.0) * _lane_heads(lg))
    k_dec = np.exp((DEC_LEN - 1.0 - t[:, None]) * _lane_heads(lg))
    return tuple(np.asarray(a, np.float32) for a in (g_pow, q_dec, k_dec))


def _hgrn_levels(t):
    levels = []
    hs = t // 2
    while hs >= SUB:
        levels.append(hs)
        hs //= 2
    return tuple(levels)


@functools.lru_cache(maxsize=None)
def _hgrn_prompt_consts(t):
    i = np.arange(t)[:, None]
    j = np.arange(t)[None, :]
    tri = (j <= i).astype(np.float32).astype(jnp.bfloat16)
    half = t // 2
    ih, jh = i[:half], j[:, :half]
    masks = []
    for hs in _hgrn_levels(t)[1:]:
        same = (ih // (2 * hs)) == (jh // (2 * hs))
        masks.append(same & ((ih % (2 * hs)) >= hs) & ((jh % (2 * hs)) < hs))
    return tri, np.stack(masks).astype(np.float32)


def _alibi_slopes():
    return np.exp2(-8.0 * (np.arange(N_HEADS, dtype=np.float64) + 1.0) / N_HEADS)


@functools.lru_cache(maxsize=None)
def _swa_prompt_bias():
    w = SWA_WINDOW
    dist = (w + np.arange(w))[:, None] - np.arange(2 * w)[None, :]
    valid = (dist >= 0) & (dist < w)
    b = np.where(valid[None], -_alibi_slopes()[:, None, None] * dist[None], MASK_VALUE)
    return np.asarray(b, np.float32)


@functools.lru_cache(maxsize=None)
def _swa_decode_bias():
    w = SWA_WINDOW
    slopes = _alibi_slopes()
    bc = np.full((2, 32, w), MASK_VALUE, np.float64)
    bn = np.full((2, 32, 8), MASK_VALUE, np.float64)
    for par in range(2):
        for h in range(N_HEADS):
            for t8 in range(8):
                t = t8 - DEC_LEN * par
                if not 0 <= t < DEC_LEN:
                    continue
                r = h * 8 + t8
                dist = w + t - np.arange(w)
                bc[par, r] = np.where((dist >= 0) & (dist < w), -slopes[h] * dist, MASK_VALUE)
                for u in range(t + 1):
                    bn[par, r, u + DEC_LEN * par] = -slopes[h] * (t - u)
    return np.asarray(bc, np.float32), np.asarray(bn, np.float32)


def _ret_prompt_kernel(z_ref, intra_ref, qdec_ref, kdec_ref, cdec_ref, nw_ref, o_ref, sout_ref, s_scr):
    i = pl.program_id(1)

    @pl.when(i == 0)
    def _():
        s_scr[...] = jnp.zeros(s_scr.shape, F32)

    q = z_ref[:, 0:256]
    k = z_ref[:, 256:512] * K_SCALE
    v = z_ref[:, 512:768]
    g = z_ref[:, 768:1024]
    lane_head = _head_id(1, (1, GROUP_WIDTH))
    kb = k.astype(BF16)
    vb = v.astype(BF16)
    s_old = s_scr[...]
    o = _dot((q * qdec_ref[...]).astype(BF16), s_old.astype(BF16))
    for h in range(N_HEADS):
        mh = lane_head == h
        sc = _dot_nt(jnp.where(mh, q, 0.0).astype(BF16), kb)
        p = (sc * intra_ref[h]).astype(BF16)
        o = o + jnp.where(mh, _dot(p, vb), 0.0)
    upd = _dot_tn((k * kdec_ref[...]).astype(BF16), vb)
    s_new = cdec_ref[...] * s_old + upd * _block_diag(1.0, F32)
    s_scr[...] = s_new
    o_ref[...] = _head_norm_gate(o, g, nw_ref[...], True).astype(BF16)

    @pl.when(i == pl.num_programs(1) - 1)
    def _():
        for h in range(N_HEADS):
            hs = slice(h * HEAD_DIM, (h + 1) * HEAD_DIM)
            sout_ref[h] = s_new[hs, hs]


def _ret_prompt(zr, l, W, bsz, seq):
    t = MIX_TILE
    nt = seq // t
    intra, q_dec, k_dec, c_dec = _ret_prompt_consts(t)
    const = lambda shape: pl.BlockSpec(shape, lambda b, i: (0,) * len(shape))
    return dict(kernel=_ret_prompt_kernel, in_specs=[pl.BlockSpec((t, 1024), lambda b, i: (b * nt + i, 0)),
                  const((N_HEADS, t, t)), const((t, GROUP_WIDTH)), const((t, GROUP_WIDTH)),
                  const((1, GROUP_WIDTH)),
                  pl.BlockSpec((None, 1, GROUP_WIDTH), lambda b, i: (l, 0, 0))],
        out_specs=[pl.BlockSpec((t, GROUP_WIDTH), lambda b, i: (b * nt + i, 0)),
                   pl.BlockSpec((None, N_HEADS, HEAD_DIM, HEAD_DIM), lambda b, i: (b, 0, 0, 0))],
        out_shape=[jax.ShapeDtypeStruct((bsz * seq, GROUP_WIDTH), BF16),
                   jax.ShapeDtypeStruct((bsz, N_HEADS, HEAD_DIM, HEAD_DIM), F32)],
                scratch=[pltpu.VMEM((GROUP_WIDTH, GROUP_WIDTH), F32)],
                args=[zr, intra, q_dec, k_dec, c_dec, W["ret_norm_w"]])


def _hgrn_gates(hf, lb):
    en = jnp.exp(-hf)
    r = 1.0 / (1.0 + en)
    log_f = jnp.log(r) + jnp.log1p(lb * en)
    k = (1.0 - lb) * (en * r)
    return log_f, k


def _hgrn_prompt_kernel(z_ref, lb_ref, nw_ref, tri_ref, lmask_ref, o_ref, sout_ref, st_scr):
    i = pl.program_id(1)
    t = z_ref.shape[0]
    half = t // 2
    w = GROUP_WIDTH

    @pl.when(i == 0)
    def _():
        st_scr[...] = jnp.zeros(st_scr.shape, F32)

    q = z_ref[:, 0:256]
    v = z_ref[:, 512:768]
    g = z_ref[:, 768:1024]
    log_f, k = _hgrn_gates(z_ref[:, 256:512], lb_ref[...])
    lane_head = _head_id(1, (1, w))
    vb = v.astype(BF16)

    tri = tri_ref[...]
    b = _dot(jnp.concatenate([tri, tri, tri], axis=1), jnp.concatenate(_split3(log_f), axis=0))

    levels = _hgrn_levels(t)
    diag_sc = [[jnp.zeros((half, half), F32) for _ in range(N_HEADS)] for _ in range(2)]
    cross_sc = [None] * N_HEADS
    for li, hs in enumerate(levels):
        nb = t // (2 * hs)
        ref = jnp.broadcast_to(b.reshape(nb, 2 * hs, w)[:, hs - 1:hs, :], (nb, 2 * hs, w)).reshape(t, w)
        e = jnp.exp(-jnp.abs(b - ref))
        qe = (q * e).astype(BF16)
        ke = (k * e).astype(BF16)
        for h in range(N_HEADS):
            mh = lane_head == h
            if li == 0:
                cross_sc[h] = _dot_nt(jnp.where(mh, qe[half:], 0), ke[:half])
            else:
                lm = lmask_ref[li - 1]
                for hf in range(2):
                    rows = slice(hf * half, (hf + 1) * half)
                    diag_sc[hf][h] = diag_sc[hf][h] + lm * _dot_nt(jnp.where(mh, qe[rows], 0), ke[rows])

    ones_bd = _block_diag(1.0, BF16)
    g3 = (t // 8, 8, w)
    sub = lax.broadcasted_iota(jnp.int32, (1, 8, 1), 1) & (SUB - 1)
    q3, k3, v3, l3 = (a.reshape(g3) for a in (q, k, v, log_f))
    o3 = jnp.zeros(g3, F32)
    bd = jnp.zeros(g3, F32)
    for d in range(SUB):
        ks = k3 if d == 0 else pltpu.roll(k3, d, axis=1)
        vs = v3 if d == 0 else pltpu.roll(v3, d, axis=1)
        p = jnp.where(sub >= d, q3 * ks * jnp.exp(bd), 0.0)
        o3 = o3 + _dot(p.reshape(t, w).astype(BF16), ones_bd).reshape(g3) * vs
        if d + 1 < SUB:
            bd = bd + (l3 if d == 0 else pltpu.roll(l3, d, axis=1))
    o = o3.reshape(t, w)

    o_lo = jnp.zeros((half, w), F32)
    o_hi = jnp.zeros((half, w), F32)
    for h in range(N_HEADS):
        mh = lane_head == h
        o_lo = o_lo + jnp.where(mh, _dot(diag_sc[0][h].astype(BF16), vb[:half]), 0.0)
        p_hi = jnp.concatenate([cross_sc[h], diag_sc[1][h]], axis=1).astype(BF16)
        o_hi = o_hi + jnp.where(mh, _dot(p_hi, vb), 0.0)
    o = o + jnp.concatenate([o_lo, o_hi], axis=0)

    st_old = st_scr[...]
    b_last = b[t - 1:t, :]
    o = o + _dot_nt((q * jnp.exp(b)).astype(BF16), st_old.astype(BF16))
    upd = _dot_tn(vb, (k * jnp.exp(b_last - b)).astype(BF16))
    st_new = st_old * jnp.exp(b_last) + upd * _block_diag(1.0, F32)
    st_scr[...] = st_new
    o_ref[...] = _head_norm_gate(o, g, nw_ref[...], False).astype(BF16)

    @pl.when(i == pl.num_programs(1) - 1)
    def _():
        for h in range(N_HEADS):
            hs = slice(h * HEAD_DIM, (h + 1) * HEAD_DIM)
            sout_ref[h] = st_new[hs, hs]


def _hgrn_prompt(zh, l, W, bsz, seq):
    t = MIX_TILE
    nt = seq // t
    tri, lmask = _hgrn_prompt_consts(t)
    const = lambda shape: pl.BlockSpec(shape, lambda b, i: (0,) * len(shape))
    vec = pl.BlockSpec((None, 1, GROUP_WIDTH), lambda b, i: (l, 0, 0))
    return dict(kernel=_hgrn_prompt_kernel, in_specs=[pl.BlockSpec((t, 1024), lambda b, i: (b * nt + i, 0)), vec, vec,
                  const(tri.shape), const(lmask.shape)],
        out_specs=[pl.BlockSpec((t, GROUP_WIDTH), lambda b, i: (b * nt + i, 0)),
                   pl.BlockSpec((None, N_HEADS, HEAD_DIM, HEAD_DIM), lambda b, i: (b, 0, 0, 0))],
        out_shape=[jax.ShapeDtypeStruct((bsz * seq, GROUP_WIDTH), BF16),
                   jax.ShapeDtypeStruct((bsz, N_HEADS, HEAD_DIM, HEAD_DIM), F32)],
                scratch=[pltpu.VMEM((GROUP_WIDTH, GROUP_WIDTH), F32)],
                args=[zh, W["hgrn_lb"], W["hgrn_norm_w"], tri, lmask])


def _rec_decode_kernel(*refs, hgrn, aliased):
    n_in = 4 if hgrn else 6
    if hgrn:
        z_ref, s_ref, lb_ref, nw_ref = refs[:n_in]
    else:
        z_ref, s_ref, gpow_ref, qdec_ref, kdec_ref, nw_ref = refs[:n_in]
    o_ref, sout_ref, qt_scr, os_scr, *more = refs[n_in + (1 if aliased else 0):]
    rows = z_ref.shape[0]
    nseq = rows // DEC_LEN
    w = GROUP_WIDTH
    q = z_ref[:, 0:256]
    v = z_ref[:, 512:768]
    g = z_ref[:, 768:1024]
    row = lax.broadcasted_iota(jnp.int32, (rows, 1), 0)
    if hgrn:
        log_f, k = _hgrn_gates(z_ref[:, 256:512], lb_ref[...])
    else:
        k = z_ref[:, 256:512] * K_SCALE

    ones_bd = _block_diag(1.0, BF16)
    g3 = (rows // 8, 8, w)
    tpos = lax.broadcasted_iota(jnp.int32, (1, 8, 1), 1) & (DEC_LEN - 1)
    q3, k3, v3 = (a.reshape(g3) for a in (q, k, v))
    o3 = jnp.zeros(g3, F32)
    if hgrn:
        l3 = log_f.reshape(g3)
        bd = jnp.zeros(g3, F32)
        b_rev = jnp.zeros(g3, F32)
    for d in range(DEC_LEN):
        valid = tpos >= d
        ks = k3 if d == 0 else pltpu.roll(k3, d, axis=1)
        vs = v3 if d == 0 else pltpu.roll(v3, d, axis=1)
        if hgrn:
            p = jnp.where(valid, q3 * ks * jnp.exp(bd), 0.0)
            o3 = o3 + _dot(p.reshape(rows, w).astype(BF16), ones_bd).reshape(g3) * vs
            bd = bd + jnp.where(valid, l3 if d == 0 else pltpu.roll(l3, d, axis=1), 0.0)
            if d > 0:
                b_rev = b_rev + jnp.where(tpos + d < DEC_LEN, pltpu.roll(l3, 8 - d, axis=1), 0.0)
        else:
            p = jnp.where(valid, q3 * ks, 0.0)
            o3 = o3 + _dot(p.reshape(rows, w).astype(BF16), ones_bd).reshape(g3) * gpow_ref[d] * vs
    o = o3.reshape(rows, w)
    if hgrn:
        dec_scr = more[0]
        b = bd.reshape(rows, w)
        b_rev = b_rev.reshape(rows, w)
        qt = q * jnp.exp(b)
        kt = k * jnp.exp(b_rev)
        dec_scr[...] = jnp.exp(b + b_rev)
        xmat, ymat = v, kt
    else:
        qt = q * qdec_ref[...]
        kt = k * kdec_ref[...]
        xmat, ymat = kt, v
    qt_scr[...] = qt
    r2 = lax.broadcasted_iota(jnp.int32, (w, w), 0)
    c2 = lax.broadcasted_iota(jnp.int32, (w, w), 1)
    eye = jnp.where(r2 == c2, 1.0, 0.0).astype(BF16)
    x_t = _dot_nt(eye, xmat.astype(BF16)).astype(BF16)
    y_heads = [ymat[:, h * HEAD_DIM:(h + 1) * HEAD_DIM] for h in range(N_HEADS)]
    sub8 = lax.broadcasted_iota(jnp.int32, (8, 1), 0)
    lg = _ret_log_gamma()

    def body(p, carry):
        r8 = pl.multiple_of(p * 8, 8)
        q8 = qt_scr[pl.ds(r8, 8), :]
        if hgrn:
            dec8 = dec_scr[pl.ds(r8, 8), :]
        o8 = None
        for par in range(2):
            s = 2 * p + par
            live = (row >> 2) == s
            outs = []
            for h in range(N_HEADS):
                hs = slice(h * HEAD_DIM, (h + 1) * HEAD_DIM)
                st = s_ref[s, hs, :]
                a8 = q8[:, hs].astype(BF16)
                if hgrn:
                    outs.append(_dot_nt(a8, st.astype(BF16)))
                    dec = dec8[par * DEC_LEN:par * DEC_LEN + 1, hs]
                else:
                    outs.append(_dot(a8, st.astype(BF16)))
                    dec = float(np.exp(DEC_LEN * lg[h]))
                ym = jnp.where(live, y_heads[h], 0.0).astype(BF16)
                sout_ref[s, hs, :] = dec * st + _dot(x_t[hs, :], ym)
            o_par = jnp.concatenate(outs, axis=1)
            o8 = o_par if par == 0 else jnp.where(sub8 < DEC_LEN, o8, o_par)
        os_scr[pl.ds(r8, 8), :] = o8
        return carry

    lax.fori_loop(0, nseq // 2, body, 0, unroll=4)
    o = o + os_scr[...]
    o_ref[...] = _head_norm_gate(o, g, nw_ref[...], not hgrn).astype(BF16)


def _rec_decode(z, states, prev, l, W, hgrn):
    rows = z.shape[0]
    rb = DEC_ROWS
    nseq = rb // DEC_LEN
    blk = lambda shape: pl.BlockSpec(shape, lambda i: (i,) + (0,) * (len(shape) - 1))
    const = lambda shape: pl.BlockSpec(shape, lambda i: (0,) * len(shape))
    vec = pl.BlockSpec((None, 1, GROUP_WIDTH), lambda i: (l, 0, 0))
    st_spec = pl.BlockSpec((None, nseq, GROUP_WIDTH, HEAD_DIM), lambda i: (l, i, 0, 0))
    scratch = [pltpu.VMEM((rb, GROUP_WIDTH), F32), pltpu.VMEM((rb, GROUP_WIDTH), F32)]
    if hgrn:
        in_specs = [blk((rb, 1024)), st_spec, vec, vec]
        args = [z, states, W["hgrn_lb"], W["hgrn_norm_w"]]
        scratch = scratch + [pltpu.VMEM((rb, GROUP_WIDTH), F32)]
    else:
        g_pow, q_dec, k_dec = _ret_decode_consts(rb)
        in_specs = [blk((rb, 1024)), st_spec, const(g_pow.shape), const(q_dec.shape), const(k_dec.shape), vec]
        args = [z, states, g_pow, q_dec, k_dec, W["ret_norm_w"]]
    aliases = {}
    if prev is not None:
        aliases = {len(args): 1}
        in_specs = in_specs + [pl.BlockSpec(memory_space=pl.ANY)]
        args = args + [prev]
    return pl.pallas_call(
        functools.partial(_rec_decode_kernel, hgrn=hgrn, aliased=prev is not None),
        grid=(rows // rb,),
        in_specs=in_specs,
        out_specs=[blk((rb, GROUP_WIDTH)), st_spec],
        out_shape=[jax.ShapeDtypeStruct((rows, GROUP_WIDTH), BF16),
                   jax.ShapeDtypeStruct(states.shape, F32)],
        input_output_aliases=aliases,
        scratch_shapes=scratch,
        compiler_params=_params(("arbitrary",)),
        name="hgrn_decode" if hgrn else "ret_decode",
    )(*args)


def _gelu_tanh(y):
    return 0.5 * y * (1.0 + jnp.tanh(math.sqrt(2.0 / math.pi) * (y + 0.044715 * (y * y * y))))


def _s5_output(xr, xi, u, cre_ref, cim_ref, d_ref, gw_ref, gb_ref):
    y = _dot(xr.astype(BF16), cre_ref[...]) - _dot(xi.astype(BF16), cim_ref[...]) + d_ref[...] * u
    y = _gelu_tanh(y)
    return y * _sigmoid(_dot(y.astype(BF16), gw_ref[...]) + gb_ref[...])


def _s5_local_scan(xr, xi, apow_ref, seg):
    sub = lax.broadcasted_iota(jnp.int32, (1, 8, 1), 1) & (seg - 1)
    d = 1
    while d < seg:
        ar = apow_ref[d - 1:d, :]
        ai = apow_ref[S5_POW + d - 1:S5_POW + d, :]
        keep = sub >= d
        sr = jnp.where(keep, pltpu.roll(xr, d, axis=1), 0.0)
        si = jnp.where(keep, pltpu.roll(xi, d, axis=1), 0.0)
        xr, xi = xr + (ar * sr - ai * si), xi + (ar * si + ai * sr)
        d *= 2
    return xr, xi


def _cmul(ar, ai, br, bi):
    return ar * br - ai * bi, ar * bi + ai * br


def _s5_prompt_kernel(u0_ref, u1_ref, bbar_ref, apow_ref, cre_ref, cim_ref, d_ref, gw_ref, gb_ref, perm_ref,
                      o_ref, sre_ref, sim_ref, xre_scr, xim_scr, cr_scr, ci_scr):
    i = pl.program_id(1)
    t = u0_ref.shape[0]
    w = S5_STATE_WIDTH
    n = S5_POW
    assert t == 8 * n

    @pl.when(i == 0)
    def _():
        cr_scr[...] = jnp.zeros(cr_scr.shape, F32)
        ci_scr[...] = jnp.zeros(ci_scr.shape, F32)

    u = jnp.concatenate(
        [jnp.concatenate([ref[pl.ds(j, 8, stride=n), :] for j in range(n)], axis=0) for ref in (u0_ref, u1_ref)],
        axis=1)
    bu = _dot(u.astype(BF16), bbar_ref[...])
    a_re = apow_ref[0:1, :]
    a_im = apow_ref[n:n + 1, :]

    xr = jnp.zeros((8, w), F32)
    xi = jnp.zeros((8, w), F32)
    for j in range(n):
        pr, pi = _cmul(a_re, a_im, xr, xi)
        xr = pr + bu[j * 8:(j + 1) * 8, 0:w]
        xi = pi + bu[j * 8:(j + 1) * 8, w:]
        xre_scr[j * 8:(j + 1) * 8, :] = xr
        xim_scr[j * 8:(j + 1) * 8, :] = xi

    an_re = apow_ref[n - 1:n, :]
    an_im = apow_ref[2 * n - 1:2 * n, :]
    sub = lax.broadcasted_iota(jnp.int32, (8, 1), 0)
    sr = cr_scr[...]
    si = ci_scr[...]
    in_re = jnp.zeros((8, w), F32)
    in_im = jnp.zeros((8, w), F32)
    for c in range(8):
        in_re = jnp.where(sub == c, sr, in_re)
        in_im = jnp.where(sub == c, si, in_im)
        pr, pi = _cmul(an_re, an_im, sr, si)
        sr = pr + xr[c:c + 1, :]
        si = pi + xi[c:c + 1, :]
    cr_scr[...] = sr
    ci_scr[...] = si

    for j in range(n):
        pr, pi = _cmul(apow_ref[j:j + 1, :], apow_ref[n + j:n + j + 1, :], in_re, in_im)
        xre_scr[j * 8:(j + 1) * 8, :] = xre_scr[j * 8:(j + 1) * 8, :] + pr
        xim_scr[j * 8:(j + 1) * 8, :] = xim_scr[j * 8:(j + 1) * 8, :] + pi

    y = _s5_output(xre_scr[...], xim_scr[...], u, cre_ref, cim_ref, d_ref, gw_ref, gb_ref)
    o_ref[...] = _dot(perm_ref[...], y.astype(BF16)).astype(BF16)

    @pl.when(i == pl.num_programs(1) - 1)
    def _():
        sre_ref[...] = sr
        sim_ref[...] = si


@functools.lru_cache(maxsize=None)
def _s5_unpermute(t):
    n = t // 8
    p = np.zeros((t, t), np.float32)
    for c in range(8):
        for j in range(n):
            p[c * n + j, j * 8 + c] = 1.0
    return p.astype(jnp.bfloat16)


def _s5_weight_specs(l):
    lay = lambda shape: pl.BlockSpec((None,) + shape, lambda *idx: (l, 0, 0))
    return [lay((GROUP_WIDTH, 2 * S5_STATE_WIDTH)), lay((2 * S5_POW, S5_STATE_WIDTH)),
            lay((S5_STATE_WIDTH, GROUP_WIDTH)), lay((S5_STATE_WIDTH, GROUP_WIDTH)),
            lay((1, GROUP_WIDTH)), lay((GROUP_WIDTH, GROUP_WIDTH)), lay((1, GROUP_WIDTH))]


def _s5_weights(W):
    return (W["s5_bbar"], W["s5_apow"], W["s5_c_re"], W["s5_c_im"], W["s5_d"], W["s5_glu_w"], W["s5_glu_b"])


def _s5_prompt(zs0, zs1, l, W, bsz, seq):
    t = MIX_TILE
    nt = seq // t
    w = S5_STATE_WIDTH
    st = pl.BlockSpec((None, 1, w), lambda b, i: (b, 0, 0))
    return dict(kernel=_s5_prompt_kernel, in_specs=[pl.BlockSpec((t, 128), lambda b, i: (b * nt + i, 0))] * 2 + _s5_weight_specs(l)
        + [pl.BlockSpec((t, t), lambda b, i: (0, 0))],
        out_specs=[pl.BlockSpec((t, GROUP_WIDTH), lambda b, i: (b * nt + i, 0)), st, st],
        out_shape=[jax.ShapeDtypeStruct((bsz * seq, GROUP_WIDTH), BF16),
                   jax.ShapeDtypeStruct((bsz, 1, w), F32), jax.ShapeDtypeStruct((bsz, 1, w), F32)],
                scratch=[pltpu.VMEM((t, w), F32), pltpu.VMEM((t, w), F32),
                        pltpu.VMEM((1, w), F32), pltpu.VMEM((1, w), F32)],
                args=[zs0, zs1, *_s5_weights(W), _s5_unpermute(t)])


def _s5_decode_kernel(u0_ref, u1_ref, x0r_ref, x0i_ref, bbar_ref, apow_ref, cre_ref, cim_ref, d_ref, gw_ref, gb_ref,
                      o_ref, xr_ref, xi_ref):
    rows = u0_ref.shape[0]
    w = S5_STATE_WIDTH
    u = jnp.concatenate([u0_ref[...], u1_ref[...]], axis=1)
    bu = _dot(u.astype(BF16), bbar_ref[...])
    first = (lax.broadcasted_iota(jnp.int32, (rows, 1), 0) & (DEC_LEN - 1)) == 0
    ar = apow_ref[0:1, :]
    ai = apow_ref[S5_POW:S5_POW + 1, :]
    x0r = x0r_ref[...]
    x0i = x0i_ref[...]
    br = bu[:, 0:w] + jnp.where(first, ar * x0r - ai * x0i, 0.0)
    bi = bu[:, w:] + jnp.where(first, ar * x0i + ai * x0r, 0.0)
    xr, xi = _s5_local_scan(br.reshape(rows // 8, 8, w), bi.reshape(rows // 8, 8, w), apow_ref, DEC_LEN)
    xr = xr.reshape(rows, w)
    xi = xi.reshape(rows, w)
    xr_ref[...] = xr
    xi_ref[...] = xi
    o_ref[...] = _s5_output(xr, xi, u, cre_ref, cim_ref, d_ref, gw_ref, gb_ref).astype(BF16)


def _s5_decode(zs0, zs1, x0r, x0i, l, W):
    rows = zs0.shape[0]
    rb = min(256, rows)
    w = S5_STATE_WIDTH
    blk = lambda width: pl.BlockSpec((rb, width), lambda i: (i, 0))
    return pl.pallas_call(
        _s5_decode_kernel,
        grid=(rows // rb,),
        in_specs=[blk(128), blk(128), blk(w), blk(w)] + _s5_weight_specs(l),
        out_specs=[blk(GROUP_WIDTH), blk(w), blk(w)],
        out_shape=[jax.ShapeDtypeStruct((rows, GROUP_WIDTH), BF16),
                   jax.ShapeDtypeStruct((rows, w), F32), jax.ShapeDtypeStruct((rows, w), F32)],
        compiler_params=_params(("arbitrary",)),
        name="s5_decode",
    )(zs0, zs1, x0r, x0i, *_s5_weights(W))


def _swa_prompt_kernel(sink_ref, za_ref, zprev_ref, bias_ref, o_ref):
    i = pl.program_id(1)
    w = SWA_WINDOW
    nblk = za_ref.shape[0] // w
    kfull = jnp.concatenate([zprev_ref[:, 256:384], za_ref[:, 256:384]], axis=0).astype(BF16)
    vfull = jnp.concatenate([zprev_ref[:, 384:512], za_ref[:, 384:512]], axis=0).astype(BF16)
    lane_half = lax.broadcasted_iota(jnp.int32, (1, w), 1) >> 6
    col = lax.broadcasted_iota(jnp.int32, (1, 2 * w), 1)
    no_prev = (col < w) & (i == 0)
    for n in range(nblk):
        kk = kfull[n * w:(n + 2) * w]
        vv = vfull[n * w:(n + 2) * w]
        for j in range(2):
            qj = za_ref[n * w:(n + 1) * w, j * w:(j + 1) * w]
            on_j = lane_half == j
            oj = jnp.zeros((w, w), F32)
            for g in range(2):
                h = j * 2 + g
                qa = qj if g == j else pltpu.roll(qj, HEAD_DIM, axis=1)
                sc = _dot_nt(jnp.where(on_j, qa, 0.0).astype(BF16), kk) * K_SCALE + bias_ref[h]
                if n == 0:
                    sc = jnp.where(no_prev, MASK_VALUE, sc)
                sink = sink_ref[h]
                m = jnp.maximum(jnp.max(sc, axis=-1, keepdims=True), sink)
                p = jnp.exp(sc - m)
                den = jnp.sum(p, axis=-1, keepdims=True) + jnp.exp(sink - m)
                og = jnp.where(on_j, _dot(p.astype(BF16), vv) / den, 0.0)
                oj = oj + (og if g == j else pltpu.roll(og, HEAD_DIM, axis=1))
            o_ref[n * w:(n + 1) * w, j * w:(j + 1) * w] = oj.astype(BF16)


def _swa_prompt(za, sinks, bsz, seq):
    t = MIX_TILE
    nt = seq // t
    per = t // SWA_WINDOW
    nb = seq // SWA_WINDOW
    bias = _swa_prompt_bias()
    return dict(kernel=_swa_prompt_kernel, in_specs=[pl.BlockSpec(memory_space=pltpu.SMEM),
                  pl.BlockSpec((t, 512), lambda b, i: (b * nt + i, 0)),
                  pl.BlockSpec((SWA_WINDOW, 512), lambda b, i: (b * nb + jnp.maximum(i * per - 1, 0), 0)),
                  pl.BlockSpec(bias.shape, lambda b, i: (0, 0, 0))],
                out_specs=pl.BlockSpec((t, GROUP_WIDTH), lambda b, i: (b * nt + i, 0)),
                out_shape=jax.ShapeDtypeStruct((bsz * seq, GROUP_WIDTH), BF16),
                scratch=[],
                args=[sinks, za, za, bias])


def _swa_decode_kernel(*refs, aliased):
    sink_ref, za_ref, kc_ref, vc_ref, bc_ref, bn_ref = refs[:6]
    o_ref, kn_ref, vn_ref, os_scr = refs[6 + (2 if aliased else 0):]
    rows = za_ref.shape[0]
    nseq = rows // DEC_LEN
    w = SWA_WINDOW
    keep = w - DEC_LEN
    lane_half = lax.broadcasted_iota(jnp.int32, (1, w), 1) >> 6
    sub8 = lax.broadcasted_iota(jnp.int32, (8, 1), 0)
    rhead = lax.broadcasted_iota(jnp.int32, (32, 1), 0) >> 3
    sink = jnp.zeros((32, 1), F32)
    for h in range(N_HEADS):
        sink = jnp.where(rhead == h, sink_ref[h], sink)

    def body(p, carry):
        r8 = pl.multiple_of(p * 8, 8)
        k8 = za_ref[pl.ds(r8, 8), 256:384]
        v8 = za_ref[pl.ds(r8, 8), 384:512]
        pieces = []
        for j in range(2):
            qj = za_ref[pl.ds(r8, 8), j * w:(j + 1) * w]
            for g in range(2):
                qa = qj if g == j else pltpu.roll(qj, HEAD_DIM, axis=1)
                pieces.append(jnp.where(lane_half == j, qa, 0.0))
        qs = jnp.concatenate(pieces, axis=0).astype(BF16)
        o8 = None
        for par in range(2):
            o_par = one_sequence(2 * p + par, par, k8, v8, qs)
            o8 = o_par if par == 0 else jnp.where(sub8 < DEC_LEN, o8, o_par)
        os_scr[pl.ds(r8, 8), :] = o8
        return carry

    def one_sequence(s, par, k8, v8, qs):
        kn_ref[s, 0:keep, :] = kc_ref[s, DEC_LEN:w, :]
        vn_ref[s, 0:keep, :] = vc_ref[s, DEC_LEN:w, :]
        kn_ref[s, keep:w, :] = k8[par * DEC_LEN:(par + 1) * DEC_LEN, :]
        vn_ref[s, keep:w, :] = v8[par * DEC_LEN:(par + 1) * DEC_LEN, :]
        sc_c = _dot_nt(qs, kc_ref[s].astype(BF16)) * K_SCALE + bc_ref[par]
        sc_n = _dot_nt(qs, k8.astype(BF16)) * K_SCALE + bn_ref[par]
        m = jnp.maximum(jnp.maximum(jnp.max(sc_c, axis=-1, keepdims=True),
                                    jnp.max(sc_n, axis=-1, keepdims=True)), sink)
        p_c = jnp.exp(sc_c - m)
        p_n = jnp.exp(sc_n - m)
        den = jnp.sum(p_c, axis=-1, keepdims=True) + jnp.sum(p_n, axis=-1, keepdims=True) + jnp.exp(sink - m)
        o32 = (_dot(p_c.astype(BF16), vc_ref[s].astype(BF16)) + _dot(p_n.astype(BF16), v8.astype(BF16))) / den
        tiles = []
        for j in range(2):
            oj = jnp.zeros((8, w), F32)
            for g in range(2):
                h = j * 2 + g
                og = jnp.where(lane_half == j, o32[h * 8:(h + 1) * 8, :], 0.0)
                oj = oj + (og if g == j else pltpu.roll(og, HEAD_DIM, axis=1))
            tiles.append(oj)
        return jnp.concatenate(tiles, axis=1)

    lax.fori_loop(0, nseq // 2, body, 0, unroll=8)
    o_ref[...] = os_scr[...].astype(BF16)


def _swa_decode(za, kc, vc, prev_k, prev_v, l, sinks):
    rows = za.shape[0]
    rb = DEC_ROWS
    nseq = rb // DEC_LEN
    bc, bn = _swa_decode_bias()
    cache = pl.BlockSpec((None, nseq, SWA_WINDOW, 128), lambda i: (l, i, 0, 0))
    in_specs = [pl.BlockSpec(memory_space=pltpu.SMEM),
                pl.BlockSpec((rb, 512), lambda i: (i, 0)), cache, cache,
                pl.BlockSpec(bc.shape, lambda i: (0, 0, 0)), pl.BlockSpec(bn.shape, lambda i: (0, 0, 0))]
    args = [sinks, za, kc, vc, bc, bn]
    aliases = {}
    if prev_k is not None:
        aliases = {len(args): 1, len(args) + 1: 2}
        in_specs = in_specs + [pl.BlockSpec(memory_space=pl.ANY)] * 2
        args = args + [prev_k, prev_v]
    return pl.pallas_call(
        functools.partial(_swa_decode_kernel, aliased=prev_k is not None),
        grid=(rows // rb,),
        in_specs=in_specs,
        out_specs=[pl.BlockSpec((rb, GROUP_WIDTH), lambda i: (i, 0)), cache, cache],
        out_shape=[jax.ShapeDtypeStruct((rows, GROUP_WIDTH), BF16),
                   jax.ShapeDtypeStruct(kc.shape, F32), jax.ShapeDtypeStruct(vc.shape, F32)],
        input_output_aliases=aliases,
        scratch_shapes=[pltpu.VMEM((rb, GROUP_WIDTH), F32)],
        compiler_params=_params(("arbitrary",)),
        name="swa_decode",
    )(*args)


def _as_list(x):
    return list(x) if isinstance(x, (list, tuple)) else [x]


def _merged_kernel(*refs, parts):
    n_in = [len(p["in_specs"]) for p in parts]
    n_out = [len(_as_list(p["out_specs"])) for p in parts]
    n_scr = [len(p["scratch"]) for p in parts]
    ins, outs, scr = refs[:sum(n_in)], refs[sum(n_in):sum(n_in) + sum(n_out)], refs[sum(n_in) + sum(n_out):]
    i = o = c = 0
    for p, ni, no, nc in zip(parts, n_in, n_out, n_scr):
        p["kernel"](*ins[i:i + ni], *outs[o:o + no], *scr[c:c + nc])
        i, o, c = i + ni, o + no, c + nc


def _prompt_mixers(zr, zs0, zs1, zh, za, l, W, bsz, seq):
    parts = [_ret_prompt(zr, l, W, bsz, seq), _s5_prompt(zs0, zs1, l, W, bsz, seq),
             _hgrn_prompt(zh, l, W, bsz, seq), _swa_prompt(za, W["swa_sinks"][l], bsz, seq)]
    outs = pl.pallas_call(
        functools.partial(_merged_kernel, parts=parts),
        grid=(bsz, seq // MIX_TILE),
        in_specs=[sp for p in parts for sp in p["in_specs"]],
        out_specs=[sp for p in parts for sp in _as_list(p["out_specs"])],
        out_shape=[sh for p in parts for sh in _as_list(p["out_shape"])],
        scratch_shapes=[sc for p in parts for sc in p["scratch"]],
        compiler_params=_params(("arbitrary", "arbitrary")),
        name="prompt_mixers",
    )(*[a for p in parts for a in p["args"]])
    o_ret, s_ret, o_s5, s_re, s_im, o_h, s_ht, o_a = outs
    return o_ret, s_ret, o_s5, s_re, s_im, o_h, s_ht, o_a


def _prepare_weights(w_in, w_out, norm_pre_mix, norm_post_mix, norm_pre_ffn, norm_post_ffn, ret_norm_w,
                     s5_a_re, s5_a_im, s5_log_step, s5_b_re, s5_b_im, s5_c_re, s5_c_im, s5_d, s5_glu_w,
                     s5_glu_b, hgrn_lower_bounds, hgrn_norm_w, swa_sinks, ffn_w_gate, ffn_w_up, ffn_w_down):
    eye = jnp.eye(16, dtype=F32)
    b_bd = lambda b: jnp.einsum("lgpc,gh->lgchp", b, eye).reshape(DEPTH, GROUP_WIDTH, S5_STATE_WIDTH)
    c_bd = lambda c: jnp.einsum("lgcp,gh->lgphc", c, eye).reshape(DEPTH, S5_STATE_WIDTH, GROUP_WIDTH)
    vec = lambda a, n: a.reshape(DEPTH, 1, n)
    bbar, apow, lb = _prep(vec(s5_a_re, S5_STATE_WIDTH), vec(s5_a_im, S5_STATE_WIDTH),
                           jnp.repeat(s5_log_step, 64, axis=1).reshape(DEPTH, 1, S5_STATE_WIDTH),
                           b_bd(s5_b_re), b_bd(s5_b_im), hgrn_lower_bounds)
    return dict(
        w_in=w_in.astype(BF16), w_out=w_out.astype(BF16),
        norm_pre_mix=vec(norm_pre_mix, D_MODEL), norm_post_mix=vec(norm_post_mix, D_MODEL),
        norm_pre_ffn=vec(norm_pre_ffn, D_MODEL), norm_post_ffn=vec(norm_post_ffn, D_MODEL),
        ret_norm_w=vec(ret_norm_w, GROUP_WIDTH), hgrn_norm_w=vec(hgrn_norm_w, GROUP_WIDTH), hgrn_lb=lb,
        s5_bbar=bbar, s5_apow=apow, s5_c_re=c_bd(s5_c_re).astype(BF16), s5_c_im=c_bd(s5_c_im).astype(BF16),
        s5_d=vec(s5_d, GROUP_WIDTH), s5_glu_w=s5_glu_w.astype(BF16), s5_glu_b=vec(s5_glu_b, GROUP_WIDTH),
        swa_sinks=swa_sinks,
        ffn_w_gate=ffn_w_gate.astype(BF16), ffn_w_up=ffn_w_up.astype(BF16), ffn_w_down=ffn_w_down.astype(BF16))


def _prompt_trunk(x, W):
    bsz, seq, _ = x.shape
    x = x.reshape(bsz * seq, D_MODEL)
    acc = [[] for _ in range(6)]
    for l in range(DEPTH):
        zr, zs0, zs1, zh, za = _in_proj(x, l, W)
        o_ret, s_ret, o_s5, s_re, s_im, o_h, s_ht, o_a = _prompt_mixers(zr, zs0, zs1, zh, za, l, W, bsz, seq)
        x = _out_ffn(x, (o_ret, o_s5, o_h, o_a), l, W)
        kv = za.reshape(bsz, seq, 512)[:, seq - SWA_WINDOW:, 256:]
        new = (s_ret, s_re.reshape(bsz, 16, 64), s_im.reshape(bsz, 16, 64), jnp.swapaxes(s_ht, -1, -2),
               kv[..., 0:128].reshape(bsz, SWA_WINDOW, 2, HEAD_DIM),
               kv[..., 128:].reshape(bsz, SWA_WINDOW, 2, HEAD_DIM))
        for a, n in zip(acc, new):
            a.append(n)
    return x.reshape(bsz, seq, D_MODEL), [jnp.stack(a) for a in acc]


def _decode_trunk(x, W, states):
    s_ret, s5_re, s5_im, s_hgrn, buf_k, buf_v = states
    bsz, seq, _ = x.shape
    w = S5_STATE_WIDTH
    x = x.reshape(bsz * seq, D_MODEL)
    st_ret = s_ret.reshape(DEPTH, bsz, GROUP_WIDTH, HEAD_DIM)
    st_hgrn = jnp.swapaxes(s_hgrn, -1, -2).reshape(DEPTH, bsz, GROUP_WIDTH, HEAD_DIM)
    kc = buf_k.reshape(DEPTH, bsz, SWA_WINDOW, 128)
    vc = buf_v.reshape(DEPTH, bsz, SWA_WINDOW, 128)
    n_ret = n_hgrn = n_k = n_v = None
    s5_new = [[], []]
    for l in range(DEPTH):
        zr, zs0, zs1, zh, za = _in_proj(x, l, W)
        o_ret, n_ret = _rec_decode(zr, st_ret, n_ret, l, W, False)
        x0r = jnp.repeat(s5_re[l].reshape(bsz, w), DEC_LEN, axis=0)
        x0i = jnp.repeat(s5_im[l].reshape(bsz, w), DEC_LEN, axis=0)
        o_s5, xr, xi = _s5_decode(zs0, zs1, x0r, x0i, l, W)
        o_h, n_hgrn = _rec_decode(zh, st_hgrn, n_hgrn, l, W, True)
        o_a, n_k, n_v = _swa_decode(za, kc, vc, n_k, n_v, l, W["swa_sinks"][l])
        x = _out_ffn(x, (o_ret, o_s5, o_h, o_a), l, W)
        for acc, xs in zip(s5_new, (xr, xi)):
            acc.append(xs.reshape(bsz, DEC_LEN, w)[:, DEC_LEN - 1].reshape(bsz, 16, 64))
    heads = (DEPTH, bsz, N_HEADS, HEAD_DIM, HEAD_DIM)
    new = [n_ret.reshape(heads), jnp.stack(s5_new[0]), jnp.stack(s5_new[1]),
           jnp.swapaxes(n_hgrn.reshape(heads), -1, -2),
           n_k.reshape(buf_k.shape), n_v.reshape(buf_v.shape)]
    return x.reshape(bsz, seq, D_MODEL), new


def kernel(x_prompt, x_sample, state_ret, state_s5_re, state_s5_im, state_hgrn, cache_swa_k, cache_swa_v,
           w_in, w_out, norm_pre_mix, norm_post_mix, norm_pre_ffn, norm_post_ffn, ret_norm_w,
           s5_a_re, s5_a_im, s5_log_step, s5_b_re, s5_b_im, s5_c_re, s5_c_im, s5_d, s5_glu_w, s5_glu_b,
           hgrn_lower_bounds, hgrn_norm_w, swa_sinks, ffn_w_gate, ffn_w_up, ffn_w_down):
    W = _prepare_weights(w_in, w_out, norm_pre_mix, norm_post_mix, norm_pre_ffn, norm_post_ffn, ret_norm_w,
                         s5_a_re, s5_a_im, s5_log_step, s5_b_re, s5_b_im, s5_c_re, s5_c_im, s5_d, s5_glu_w,
                         s5_glu_b, hgrn_lower_bounds, hgrn_norm_w, swa_sinks, ffn_w_gate, ffn_w_up, ffn_w_down)
    y_prompt, p_states = _prompt_trunk(x_prompt, W)
    y_sample, s_states = _decode_trunk(x_sample, W, (state_ret, state_s5_re, state_s5_im, state_hgrn,
                                                     cache_swa_k, cache_swa_v))
    return (y_prompt, y_sample, *p_states, *s_states)
```

```python
import functools
import math

import numpy as np
import jax
import jax.numpy as jnp
from jax import lax
from jax.experimental import pallas as pl
from jax.experimental.pallas import tpu as pltpu

F32 = jnp.float32
BF16 = jnp.bfloat16

D_MODEL = 1024
HEAD_DIM = 64
GROUP_WIDTH = 256
N_HEADS = 4
DEPTH = 4
SWA_WINDOW = 128
S5_STATE_WIDTH = 1024
FFN_HIDDEN = 2816
FFN_CHUNK = 256
IN_COLS = 2816
NORM_EPS = 1e-6
MASK_VALUE = -1e30
K_SCALE = HEAD_DIM ** -0.5
SUB = 4
VMEM_LIMIT = 56 * 1024 * 1024

ROW_TILE = 512
MIX_TILE = 256
DEC_ROWS = 64
DEC_LEN = 4
S5_POW = 32

_NT = (((1,), (1,)), ((), ()))
_TN = (((0,), (0,)), ((), ()))


def _dot(a, b):
    return jnp.dot(a, b, preferred_element_type=F32)


def _dot_nt(a, b):
    return lax.dot_general(a, b, _NT, preferred_element_type=F32)


def _dot_tn(a, b):
    return lax.dot_general(a, b, _TN, preferred_element_type=F32)


def _params(sem):
    return pltpu.CompilerParams(dimension_semantics=sem, vmem_limit_bytes=VMEM_LIMIT)


def _rms(x, w):
    return x * lax.rsqrt(jnp.mean(x * x, axis=-1, keepdims=True) + NORM_EPS) * w


def _sigmoid(x):
    return 1.0 / (1.0 + jnp.exp(-x))


def _split2(x):
    hi = x.astype(BF16)
    lo = (x - hi.astype(F32)).astype(BF16)
    return hi, lo


def _split3(x):
    hi = x.astype(BF16)
    r1 = x - hi.astype(F32)
    mid = r1.astype(BF16)
    lo = (r1 - mid.astype(F32)).astype(BF16)
    return hi, mid, lo


def _head_id(axis, shape):
    return lax.broadcasted_iota(jnp.int32, shape, axis) >> 6


def _block_diag(val, dtype):
    r = _head_id(0, (GROUP_WIDTH, GROUP_WIDTH))
    c = _head_id(1, (GROUP_WIDTH, GROUP_WIDTH))
    return jnp.where(r == c, val, 0.0).astype(dtype)


def _head_sum(x, ones_bd):
    hi, lo = _split2(x)
    return _dot(jnp.concatenate([hi, lo], axis=1), jnp.concatenate([ones_bd, ones_bd], axis=0))


def _head_norm_gate(o, g, w, center):
    mean_bd = _block_diag(1.0 / HEAD_DIM, BF16)
    if center:
        o = o - _head_sum(o, mean_bd)
    var = _head_sum(o * o, mean_bd)
    return o * lax.rsqrt(var + NORM_EPS) * w * (g * _sigmoid(g))


def _prep_kernel(are_ref, aim_ref, ls_ref, bre_ref, bim_ref, lbw_ref, bbar_ref, apow_ref, lb_ref):
    l = pl.program_id(0)
    a_re = are_ref[...]
    a_im = aim_ref[...]
    dt = jnp.exp(ls_ref[...])
    mag = jnp.exp(a_re * dt)
    ab_re = mag * jnp.cos(a_im * dt)
    ab_im = mag * jnp.sin(a_im * dt)
    den = a_re * a_re + a_im * a_im
    g_re = ((ab_re - 1.0) * a_re + ab_im * a_im) / den
    g_im = (ab_im * a_re - (ab_re - 1.0) * a_im) / den
    b_re = bre_ref[...]
    b_im = bim_ref[...]
    bbar_ref[:, 0:S5_STATE_WIDTH] = (g_re * b_re - g_im * b_im).astype(BF16)
    bbar_ref[:, S5_STATE_WIDTH:] = (g_re * b_im + g_im * b_re).astype(BF16)
    p_re, p_im = ab_re, ab_im
    for r in range(S5_POW):
        apow_ref[r:r + 1, :] = p_re
        apow_ref[S5_POW + r:S5_POW + r + 1, :] = p_im
        p_re, p_im = p_re * ab_re - p_im * ab_im, p_re * ab_im + p_im * ab_re
    w = lbw_ref[...]
    e = jnp.exp(w - jnp.max(w, axis=0, keepdims=True))
    sm = e / jnp.sum(e, axis=0, keepdims=True)
    row = lax.broadcasted_iota(jnp.int32, sm.shape, 0)
    lb_ref[...] = jnp.sum(jnp.where((row >= 1) & (row <= l), sm, 0.0), axis=0, keepdims=True)


def _prep(a_re, a_im, log_step, b_re_bd, b_im_bd, lower_bounds):
    vec = pl.BlockSpec((None, 1, S5_STATE_WIDTH), lambda l: (l, 0, 0))
    mat = pl.BlockSpec((None, GROUP_WIDTH, S5_STATE_WIDTH), lambda l: (l, 0, 0))
    return pl.pallas_call(
        _prep_kernel,
        grid=(DEPTH,),
        in_specs=[vec, vec, vec, mat, mat, pl.BlockSpec((DEPTH, GROUP_WIDTH), lambda l: (0, 0))],
        out_specs=[pl.BlockSpec((None, GROUP_WIDTH, 2 * S5_STATE_WIDTH), lambda l: (l, 0, 0)),
                   pl.BlockSpec((None, 2 * S5_POW, S5_STATE_WIDTH), lambda l: (l, 0, 0)),
                   pl.BlockSpec((None, 1, GROUP_WIDTH), lambda l: (l, 0, 0))],
        out_shape=[jax.ShapeDtypeStruct((DEPTH, GROUP_WIDTH, 2 * S5_STATE_WIDTH), BF16),
                   jax.ShapeDtypeStruct((DEPTH, 2 * S5_POW, S5_STATE_WIDTH), F32),
                   jax.ShapeDtypeStruct((DEPTH, 1, GROUP_WIDTH), F32)],
        compiler_params=_params(("arbitrary",)),
        name="prep_weights",
    )(a_re, a_im, log_step, b_re_bd, b_im_bd, lower_bounds)


def _in_kernel(x_ref, nw_ref, w_ref, zr_ref, zs0_ref, zs1_ref, zh_ref, za_ref):
    h = _rms(x_ref[...], nw_ref[...]).astype(BF16)
    zr_ref[...] = _dot(h, w_ref[:, 0:1024])
    zs0_ref[...] = _dot(h, w_ref[:, 1024:1152])
    zs1_ref[...] = _dot(h, w_ref[:, 1152:1280])
    zh_ref[...] = _dot(h, w_ref[:, 1280:2304])
    za_ref[...] = _dot(h, w_ref[:, 2304:2816])


def _in_proj(x, l, W):
    n = x.shape[0]
    tm = min(ROW_TILE, n)
    widths = (1024, 128, 128, 1024, 512)
    return pl.pallas_call(
        _in_kernel,
        grid=(n // tm,),
        in_specs=[pl.BlockSpec((tm, D_MODEL), lambda i: (i, 0)),
                  pl.BlockSpec((None, 1, D_MODEL), lambda i: (l, 0, 0)),
                  pl.BlockSpec((None, D_MODEL, IN_COLS), lambda i: (l, 0, 0))],
        out_specs=[pl.BlockSpec((tm, w), lambda i: (i, 0)) for w in widths],
        out_shape=[jax.ShapeDtypeStruct((n, w), F32) for w in widths],
        compiler_params=_params(("arbitrary",)),
        name="in_proj",
    )(x, W["norm_pre_mix"], W["w_in"])


def _out_kernel(x_ref, o1_ref, o2_ref, o3_ref, o4_ref, wo_ref, npm_ref, npf_ref, npo_ref,
                wg_ref, wu_ref, wd_ref, y_ref):
    m = _dot(o1_ref[...], wo_ref[0:256, :])
    m = m + _dot(o2_ref[...], wo_ref[256:512, :])
    m = m + _dot(o3_ref[...], wo_ref[512:768, :])
    m = m + _dot(o4_ref[...], wo_ref[768:1024, :])
    x1 = x_ref[...] + _rms(m, npm_ref[...])
    h = _rms(x1, npf_ref[...]).astype(BF16)
    acc = jnp.zeros(x1.shape, F32)
    for c in range(FFN_HIDDEN // FFN_CHUNK):
        cs = slice(c * FFN_CHUNK, (c + 1) * FFN_CHUNK)
        gate = _dot(h, wg_ref[:, cs])
        up = _dot(h, wu_ref[:, cs])
        f = (gate * _sigmoid(gate) * up).astype(BF16)
        acc = acc + _dot(f, wd_ref[cs, :])
    y_ref[...] = x1 + _rms(acc, npo_ref[...])


def _out_ffn(x, outs, l, W):
    n = x.shape[0]
    tm = min(ROW_TILE, n)
    row = lambda w: pl.BlockSpec((tm, w), lambda i: (i, 0))
    vec = pl.BlockSpec((None, 1, D_MODEL), lambda i: (l, 0, 0))
    once = dict(pipeline_mode=pl.Buffered(1))
    return pl.pallas_call(
        _out_kernel,
        grid=(n // tm,),
        in_specs=[row(D_MODEL)] + [row(GROUP_WIDTH)] * 4 + [
            pl.BlockSpec((None, D_MODEL, D_MODEL), lambda i: (l, 0, 0), **once),
            vec, vec, vec,
            pl.BlockSpec((None, D_MODEL, FFN_HIDDEN), lambda i: (l, 0, 0), **once),
            pl.BlockSpec((None, D_MODEL, FFN_HIDDEN), lambda i: (l, 0, 0), **once),
            pl.BlockSpec((None, FFN_HIDDEN, D_MODEL), lambda i: (l, 0, 0), **once)],
        out_specs=row(D_MODEL),
        out_shape=jax.ShapeDtypeStruct((n, D_MODEL), F32),
        compiler_params=_params(("arbitrary",)),
        name="out_ffn",
    )(x, *outs, W["w_out"], W["norm_post_mix"], W["norm_pre_ffn"], W["norm_post_ffn"],
      W["ffn_w_gate"], W["ffn_w_up"], W["ffn_w_down"])


def _ret_log_gamma():
    return np.log1p(-np.exp2(-5.0 - np.arange(N_HEADS, dtype=np.float64)))


def _lane_heads(v):
    return np.repeat(np.asarray(v, np.float64), HEAD_DIM)[None, :]


@functools.lru_cache(maxsize=None)
def _ret_prompt_consts(t):
    lg = _ret_log_gamma()
    idx = np.arange(t, dtype=np.float64)
    diff = idx[:, None] - idx[None, :]
    intra = np.where(diff >= 0, np.exp(np.maximum(diff, 0.0)[None] * lg[:, None, None]), 0.0)
    q_dec = np.exp((idx[:, None] + 1.0) * _lane_heads(lg))
    k_dec = np.exp((t - 1.0 - idx[:, None]) * _lane_heads(lg))
    c_dec = np.exp(t * _lane_heads(lg))
    return tuple(np.asarray(a, np.float32) for a in (intra, q_dec, k_dec, c_dec))


@functools.lru_cache(maxsize=None)
def _ret_decode_consts(rows):
    lg = _ret_log_gamma()
    t = (np.arange(rows) % DEC_LEN).astype(np.float64)
    g_pow = np.stack([np.exp(d * _lane_heads(lg)) for d in range(DEC_LEN)])
    q_dec = np.exp((t[:, None] + 1.0) * _lane_heads(lg))
    k_dec = np.exp((DEC_LEN - 1.0 - t[:, None]) * _lane_heads(lg))
    return tuple(np.asarray(a, np.float32) for a in (g_pow, q_dec, k_dec))


def _hgrn_levels(t):
    levels = []
    hs = t // 2
    while hs >= SUB:
        levels.append(hs)
        hs //= 2
    return tuple(levels)


@functools.lru_cache(maxsize=None)
def _hgrn_prompt_consts(t):
    i = np.arange(t)[:, None]
    j = np.arange(t)[None, :]
    tri = (j <= i).astype(np.float32).astype(jnp.bfloat16)
    half = t // 2
    ih, jh = i[:half], j[:, :half]
    masks = []
    for hs in _hgrn_levels(t)[1:]:
        same = (ih // (2 * hs)) == (jh // (2 * hs))
        masks.append(same & ((ih % (2 * hs)) >= hs) & ((jh % (2 * hs)) < hs))
    return tri, np.stack(masks).astype(np.float32)


def _alibi_slopes():
    return np.exp2(-8.0 * (np.arange(N_HEADS, dtype=np.float64) + 1.0) / N_HEADS)


@functools.lru_cache(maxsize=None)
def _swa_prompt_bias():
    w = SWA_WINDOW
    dist = (w + np.arange(w))[:, None] - np.arange(2 * w)[None, :]
    valid = (dist >= 0) & (dist < w)
    b = np.where(valid[None], -_alibi_slopes()[:, None, None] * dist[None], MASK_VALUE)
    return np.asarray(b, np.float32)


@functools.lru_cache(maxsize=None)
def _swa_decode_bias():
    w = SWA_WINDOW
    slopes = _alibi_slopes()
    bc = np.full((2, 32, w), MASK_VALUE, np.float64)
    bn = np.full((2, 32, 8), MASK_VALUE, np.float64)
    for par in range(2):
        for h in range(N_HEADS):
            for t8 in range(8):
                t = t8 - DEC_LEN * par
                if not 0 <= t < DEC_LEN:
                    continue
                r = h * 8 + t8
                dist = w + t - np.arange(w)
                bc[par, r] = np.where((dist >= 0) & (dist < w), -slopes[h] * dist, MASK_VALUE)
                for u in range(t + 1):
                    bn[par, r, u + DEC_LEN * par] = -slopes[h] * (t - u)
    return np.asarray(bc, np.float32), np.asarray(bn, np.float32)


def _ret_prompt_kernel(z_ref, intra_ref, qdec_ref, kdec_ref, cdec_ref, nw_ref, o_ref, sout_ref, s_scr):
    i = pl.program_id(1)

    @pl.when(i == 0)
    def _():
        s_scr[...] = jnp.zeros(s_scr.shape, F32)

    q = z_ref[:, 0:256]
    k = z_ref[:, 256:512] * K_SCALE
    v = z_ref[:, 512:768]
    g = z_ref[:, 768:1024]
    lane_head = _head_id(1, (1, GROUP_WIDTH))
    kb = k.astype(BF16)
    vb = v.astype(BF16)
    s_old = s_scr[...]
    o = _dot((q * qdec_ref[...]).astype(BF16), s_old.astype(BF16))
    for h in range(N_HEADS):
        mh = lane_head == h
        sc = _dot_nt(jnp.where(mh, q, 0.0).astype(BF16), kb)
        p = (sc * intra_ref[h]).astype(BF16)
        o = o + jnp.where(mh, _dot(p, vb), 0.0)
    upd = _dot_tn((k * kdec_ref[...]).astype(BF16), vb)
    s_new = cdec_ref[...] * s_old + upd * _block_diag(1.0, F32)
    s_scr[...] = s_new
    o_ref[...] = _head_norm_gate(o, g, nw_ref[...], True).astype(BF16)

    @pl.when(i == pl.num_programs(1) - 1)
    def _():
        for h in range(N_HEADS):
            hs = slice(h * HEAD_DIM, (h + 1) * HEAD_DIM)
            sout_ref[h] = s_new[hs, hs]


def _ret_prompt(zr, l, W, bsz, seq):
    t = MIX_TILE
    nt = seq // t
    intra, q_dec, k_dec, c_dec = _ret_prompt_consts(t)
    const = lambda shape: pl.BlockSpec(shape, lambda b, i: (0,) * len(shape))
    return dict(kernel=_ret_prompt_kernel, in_specs=[pl.BlockSpec((t, 1024), lambda b, i: (b * nt + i, 0)),
                  const((N_HEADS, t, t)), const((t, GROUP_WIDTH)), const((t, GROUP_WIDTH)),
                  const((1, GROUP_WIDTH)),
                  pl.BlockSpec((None, 1, GROUP_WIDTH), lambda b, i: (l, 0, 0))],
        out_specs=[pl.BlockSpec((t, GROUP_WIDTH), lambda b, i: (b * nt + i, 0)),
                   pl.BlockSpec((None, N_HEADS, HEAD_DIM, HEAD_DIM), lambda b, i: (b, 0, 0, 0))],
        out_shape=[jax.ShapeDtypeStruct((bsz * seq, GROUP_WIDTH), BF16),
                   jax.ShapeDtypeStruct((bsz, N_HEADS, HEAD_DIM, HEAD_DIM), F32)],
                scratch=[pltpu.VMEM((GROUP_WIDTH, GROUP_WIDTH), F32)],
                args=[zr, intra, q_dec, k_dec, c_dec, W["ret_norm_w"]])


def _hgrn_gates(hf, lb):
    en = jnp.exp(-hf)
    r = 1.0 / (1.0 + en)
    log_f = jnp.log(r) + jnp.log1p(lb * en)
    k = (1.0 - lb) * (en * r)
    return log_f, k


def _hgrn_prompt_kernel(z_ref, lb_ref, nw_ref, tri_ref, lmask_ref, o_ref, sout_ref, st_scr):
    i = pl.program_id(1)
    t = z_ref.shape[0]
    half = t // 2
    w = GROUP_WIDTH

    @pl.when(i == 0)
    def _():
        st_scr[...] = jnp.zeros(st_scr.shape, F32)

    q = z_ref[:, 0:256]
    v = z_ref[:, 512:768]
    g = z_ref[:, 768:1024]
    log_f, k = _hgrn_gates(z_ref[:, 256:512], lb_ref[...])
    lane_head = _head_id(1, (1, w))
    vb = v.astype(BF16)

    tri = tri_ref[...]
    b = _dot(jnp.concatenate([tri, tri, tri], axis=1), jnp.concatenate(_split3(log_f), axis=0))

    levels = _hgrn_levels(t)
    diag_sc = [[jnp.zeros((half, half), F32) for _ in range(N_HEADS)] for _ in range(2)]
    cross_sc = [None] * N_HEADS
    for li, hs in enumerate(levels):
        nb = t // (2 * hs)
        ref = jnp.broadcast_to(b.reshape(nb, 2 * hs, w)[:, hs - 1:hs, :], (nb, 2 * hs, w)).reshape(t, w)
        e = jnp.exp(-jnp.abs(b - ref))
        qe = (q * e).astype(BF16)
        ke = (k * e).astype(BF16)
        for h in range(N_HEADS):
            mh = lane_head == h
            if li == 0:
                cross_sc[h] = _dot_nt(jnp.where(mh, qe[half:], 0), ke[:half])
            else:
                lm = lmask_ref[li - 1]
                for hf in range(2):
                    rows = slice(hf * half, (hf + 1) * half)
                    diag_sc[hf][h] = diag_sc[hf][h] + lm * _dot_nt(jnp.where(mh, qe[rows], 0), ke[rows])

    ones_bd = _block_diag(1.0, BF16)
    g3 = (t // 8, 8, w)
    sub = lax.broadcasted_iota(jnp.int32, (1, 8, 1), 1) & (SUB - 1)
    q3, k3, v3, l3 = (a.reshape(g3) for a in (q, k, v, log_f))
    o3 = jnp.zeros(g3, F32)
    bd = jnp.zeros(g3, F32)
    for d in range(SUB):
        ks = k3 if d == 0 else pltpu.roll(k3, d, axis=1)
        vs = v3 if d == 0 else pltpu.roll(v3, d, axis=1)
        p = jnp.where(sub >= d, q3 * ks * jnp.exp(bd), 0.0)
        o3 = o3 + _dot(p.reshape(t, w).astype(BF16), ones_bd).reshape(g3) * vs
        if d + 1 < SUB:
            bd = bd + (l3 if d == 0 else pltpu.roll(l3, d, axis=1))
    o = o3.reshape(t, w)

    o_lo = jnp.zeros((half, w), F32)
    o_hi = jnp.zeros((half, w), F32)
    for h in range(N_HEADS):
        mh = lane_head == h
        o_lo = o_lo + jnp.where(mh, _dot(diag_sc[0][h].astype(BF16), vb[:half]), 0.0)
        p_hi = jnp.concatenate([cross_sc[h], diag_sc[1][h]], axis=1).astype(BF16)
        o_hi = o_hi + jnp.where(mh, _dot(p_hi, vb), 0.0)
    o = o + jnp.concatenate([o_lo, o_hi], axis=0)

    st_old = st_scr[...]
    b_last = b[t - 1:t, :]
    o = o + _dot_nt((q * jnp.exp(b)).astype(BF16), st_old.astype(BF16))
    upd = _dot_tn(vb, (k * jnp.exp(b_last - b)).astype(BF16))
    st_new = st_old * jnp.exp(b_last) + upd * _block_diag(1.0, F32)
    st_scr[...] = st_new
    o_ref[...] = _head_norm_gate(o, g, nw_ref[...], False).astype(BF16)

    @pl.when(i == pl.num_programs(1) - 1)
    def _():
        for h in range(N_HEADS):
            hs = slice(h * HEAD_DIM, (h + 1) * HEAD_DIM)
            sout_ref[h] = st_new[hs, hs]


def _hgrn_prompt(zh, l, W, bsz, seq):
    t = MIX_TILE
    nt = seq // t
    tri, lmask = _hgrn_prompt_consts(t)
    const = lambda shape: pl.BlockSpec(shape, lambda b, i: (0,) * len(shape))
    vec = pl.BlockSpec((None, 1, GROUP_WIDTH), lambda b, i: (l, 0, 0))
    return dict(kernel=_hgrn_prompt_kernel, in_specs=[pl.BlockSpec((t, 1024), lambda b, i: (b * nt + i, 0)), vec, vec,
                  const(tri.shape), const(lmask.shape)],
        out_specs=[pl.BlockSpec((t, GROUP_WIDTH), lambda b, i: (b * nt + i, 0)),
                   pl.BlockSpec((None, N_HEADS, HEAD_DIM, HEAD_DIM), lambda b, i: (b, 0, 0, 0))],
        out_shape=[jax.ShapeDtypeStruct((bsz * seq, GROUP_WIDTH), BF16),
                   jax.ShapeDtypeStruct((bsz, N_HEADS, HEAD_DIM, HEAD_DIM), F32)],
                scratch=[pltpu.VMEM((GROUP_WIDTH, GROUP_WIDTH), F32)],
                args=[zh, W["hgrn_lb"], W["hgrn_norm_w"], tri, lmask])


def _rec_decode_kernel(*refs, hgrn, aliased):
    n_in = 4 if hgrn else 6
    if hgrn:
        z_ref, s_ref, lb_ref, nw_ref = refs[:n_in]
    else:
        z_ref, s_ref, gpow_ref, qdec_ref, kdec_ref, nw_ref = refs[:n_in]
    o_ref, sout_ref, qt_scr, os_scr, *more = refs[n_in + (1 if aliased else 0):]
    rows = z_ref.shape[0]
    nseq = rows // DEC_LEN
    w = GROUP_WIDTH
    q = z_ref[:, 0:256]
    v = z_ref[:, 512:768]
    g = z_ref[:, 768:1024]
    row = lax.broadcasted_iota(jnp.int32, (rows, 1), 0)
    if hgrn:
        log_f, k = _hgrn_gates(z_ref[:, 256:512], lb_ref[...])
    else:
        k = z_ref[:, 256:512] * K_SCALE

    ones_bd = _block_diag(1.0, BF16)
    g3 = (rows // 8, 8, w)
    tpos = lax.broadcasted_iota(jnp.int32, (1, 8, 1), 1) & (DEC_LEN - 1)
    q3, k3, v3 = (a.reshape(g3) for a in (q, k, v))
    o3 = jnp.zeros(g3, F32)
    if hgrn:
        l3 = log_f.reshape(g3)
        bd = jnp.zeros(g3, F32)
        b_rev = jnp.zeros(g3, F32)
    for d in range(DEC_LEN):
        valid = tpos >= d
        ks = k3 if d == 0 else pltpu.roll(k3, d, axis=1)
        vs = v3 if d == 0 else pltpu.roll(v3, d, axis=1)
        if hgrn:
            p = jnp.where(valid, q3 * ks * jnp.exp(bd), 0.0)
            o3 = o3 + _dot(p.reshape(rows, w).astype(BF16), ones_bd).reshape(g3) * vs
            bd = bd + jnp.where(valid, l3 if d == 0 else pltpu.roll(l3, d, axis=1), 0.0)
            if d > 0:
                b_rev = b_rev + jnp.where(tpos + d < DEC_LEN, pltpu.roll(l3, 8 - d, axis=1), 0.0)
        else:
            p = jnp.where(valid, q3 * ks, 0.0)
            o3 = o3 + _dot(p.reshape(rows, w).astype(BF16), ones_bd).reshape(g3) * gpow_ref[d] * vs
    o = o3.reshape(rows, w)
    if hgrn:
        dec_scr = more[0]
        b = bd.reshape(rows, w)
        b_rev = b_rev.reshape(rows, w)
        qt = q * jnp.exp(b)
        kt = k * jnp.exp(b_rev)
        dec_scr[...] = jnp.exp(b + b_rev)
        xmat, ymat = v, kt
    else:
        qt = q * qdec_ref[...]
        kt = k * kdec_ref[...]
        xmat, ymat = kt, v
    qt_scr[...] = qt
    r2 = lax.broadcasted_iota(jnp.int32, (w, w), 0)
    c2 = lax.broadcasted_iota(jnp.int32, (w, w), 1)
    eye = jnp.where(r2 == c2, 1.0, 0.0).astype(BF16)
    x_t = _dot_nt(eye, xmat.astype(BF16)).astype(BF16)
    y_heads = [ymat[:, h * HEAD_DIM:(h + 1) * HEAD_DIM] for h in range(N_HEADS)]
    sub8 = lax.broadcasted_iota(jnp.int32, (8, 1), 0)
    lg = _ret_log_gamma()

    def body(p, carry):
        r8 = pl.multiple_of(p * 8, 8)
        q8 = qt_scr[pl.ds(r8, 8), :]
        if hgrn:
            dec8 = dec_scr[pl.ds(r8, 8), :]
        o8 = None
        for par in range(2):
            s = 2 * p + par
            live = (row >> 2) == s
            outs = []
            for h in range(N_HEADS):
                hs = slice(h * HEAD_DIM, (h + 1) * HEAD_DIM)
                st = s_ref[s, hs, :]
                a8 = q8[:, hs].astype(BF16)
                if hgrn:
                    outs.append(_dot_nt(a8, st.astype(BF16)))
                    dec = dec8[par * DEC_LEN:par * DEC_LEN + 1, hs]
                else:
                    outs.append(_dot(a8, st.astype(BF16)))
                    dec = float(np.exp(DEC_LEN * lg[h]))
                ym = jnp.where(live, y_heads[h], 0.0).astype(BF16)
                sout_ref[s, hs, :] = dec * st + _dot(x_t[hs, :], ym)
            o_par = jnp.concatenate(outs, axis=1)
            o8 = o_par if par == 0 else jnp.where(sub8 < DEC_LEN, o8, o_par)
        os_scr[pl.ds(r8, 8), :] = o8
        return carry

    lax.fori_loop(0, nseq // 2, body, 0, unroll=4)
    o = o + os_scr[...]
    o_ref[...] = _head_norm_gate(o, g, nw_ref[...], not hgrn).astype(BF16)


def _rec_decode(z, states, prev, l, W, hgrn):
    rows = z.shape[0]
    rb = DEC_ROWS
    nseq = rb // DEC_LEN
    blk = lambda shape: pl.BlockSpec(shape, lambda i: (i,) + (0,) * (len(shape) - 1))
    const = lambda shape: pl.BlockSpec(shape, lambda i: (0,) * len(shape))
    vec = pl.BlockSpec((None, 1, GROUP_WIDTH), lambda i: (l, 0, 0))
    st_spec = pl.BlockSpec((None, nseq, GROUP_WIDTH, HEAD_DIM), lambda i: (l, i, 0, 0))
    scratch = [pltpu.VMEM((rb, GROUP_WIDTH), F32), pltpu.VMEM((rb, GROUP_WIDTH), F32)]
    if hgrn:
        in_specs = [blk((rb, 1024)), st_spec, vec, vec]
        args = [z, states, W["hgrn_lb"], W["hgrn_norm_w"]]
        scratch = scratch + [pltpu.VMEM((rb, GROUP_WIDTH), F32)]
    else:
        g_pow, q_dec, k_dec = _ret_decode_consts(rb)
        in_specs = [blk((rb, 1024)), st_spec, const(g_pow.shape), const(q_dec.shape), const(k_dec.shape), vec]
        args = [z, states, g_pow, q_dec, k_dec, W["ret_norm_w"]]
    aliases = {}
    if prev is not None:
        aliases = {len(args): 1}
        in_specs = in_specs + [pl.BlockSpec(memory_space=pl.ANY)]
        args = args + [prev]
    return dict(kernel=functools.partial(_rec_decode_kernel, hgrn=hgrn, aliased=prev is not None),
                in_specs=in_specs, out_specs=[blk((rb, GROUP_WIDTH)), st_spec],
                out_shape=[jax.ShapeDtypeStruct((rows, GROUP_WIDTH), BF16),
                           jax.ShapeDtypeStruct(states.shape, F32)],
                aliases=aliases, scratch=scratch, args=args)


def _gelu_tanh(y):
    return 0.5 * y * (1.0 + jnp.tanh(math.sqrt(2.0 / math.pi) * (y + 0.044715 * (y * y * y))))


def _s5_output(xr, xi, u, cre_ref, cim_ref, d_ref, gw_ref, gb_ref):
    y = _dot(xr.astype(BF16), cre_ref[...]) - _dot(xi.astype(BF16), cim_ref[...]) + d_ref[...] * u
    y = _gelu_tanh(y)
    return y * _sigmoid(_dot(y.astype(BF16), gw_ref[...]) + gb_ref[...])


def _s5_local_scan(xr, xi, apow_ref, seg):
    sub = lax.broadcasted_iota(jnp.int32, (1, 8, 1), 1) & (seg - 1)
    d = 1
    while d < seg:
        ar = apow_ref[d - 1:d, :]
        ai = apow_ref[S5_POW + d - 1:S5_POW + d, :]
        keep = sub >= d
        sr = jnp.where(keep, pltpu.roll(xr, d, axis=1), 0.0)
        si = jnp.where(keep, pltpu.roll(xi, d, axis=1), 0.0)
        xr, xi = xr + (ar * sr - ai * si), xi + (ar * si + ai * sr)
        d *= 2
    return xr, xi


def _cmul(ar, ai, br, bi):
    return ar * br - ai * bi, ar * bi + ai * br


def _s5_prompt_kernel(u0_ref, u1_ref, bbar_ref, apow_ref, cre_ref, cim_ref, d_ref, gw_ref, gb_ref, perm_ref,
                      o_ref, sre_ref, sim_ref, xre_scr, xim_scr, cr_scr, ci_scr):
    i = pl.program_id(1)
    t = u0_ref.shape[0]
    w = S5_STATE_WIDTH
    n = S5_POW
    assert t == 8 * n

    @pl.when(i == 0)
    def _():
        cr_scr[...] = jnp.zeros(cr_scr.shape, F32)
        ci_scr[...] = jnp.zeros(ci_scr.shape, F32)

    u = jnp.concatenate(
        [jnp.concatenate([ref[pl.ds(j, 8, stride=n), :] for j in range(n)], axis=0) for ref in (u0_ref, u1_ref)],
        axis=1)
    bu = _dot(u.astype(BF16), bbar_ref[...])
    a_re = apow_ref[0:1, :]
    a_im = apow_ref[n:n + 1, :]

    xr = jnp.zeros((8, w), F32)
    xi = jnp.zeros((8, w), F32)
    for j in range(n):
        pr, pi = _cmul(a_re, a_im, xr, xi)
        xr = pr + bu[j * 8:(j + 1) * 8, 0:w]
        xi = pi + bu[j * 8:(j + 1) * 8, w:]
        xre_scr[j * 8:(j + 1) * 8, :] = xr
        xim_scr[j * 8:(j + 1) * 8, :] = xi

    an_re = apow_ref[n - 1:n, :]
    an_im = apow_ref[2 * n - 1:2 * n, :]
    sub = lax.broadcasted_iota(jnp.int32, (8, 1), 0)
    sr = cr_scr[...]
    si = ci_scr[...]
    in_re = jnp.zeros((8, w), F32)
    in_im = jnp.zeros((8, w), F32)
    for c in range(8):
        in_re = jnp.where(sub == c, sr, in_re)
        in_im = jnp.where(sub == c, si, in_im)
        pr, pi = _cmul(an_re, an_im, sr, si)
        sr = pr + xr[c:c + 1, :]
        si = pi + xi[c:c + 1, :]
    cr_scr[...] = sr
    ci_scr[...] = si

    for j in range(n):
        pr, pi = _cmul(apow_ref[j:j + 1, :], apow_ref[n + j:n + j + 1, :], in_re, in_im)
        xre_scr[j * 8:(j + 1) * 8, :] = xre_scr[j * 8:(j + 1) * 8, :] + pr
        xim_scr[j * 8:(j + 1) * 8, :] = xim_scr[j * 8:(j + 1) * 8, :] + pi

    y = _s5_output(xre_scr[...], xim_scr[...], u, cre_ref, cim_ref, d_ref, gw_ref, gb_ref)
    o_ref[...] = _dot(perm_ref[...], y.astype(BF16)).astype(BF16)

    @pl.when(i == pl.num_programs(1) - 1)
    def _():
        sre_ref[...] = sr
        sim_ref[...] = si


@functools.lru_cache(maxsize=None)
def _s5_unpermute(t):
    n = t // 8
    p = np.zeros((t, t), np.float32)
    for c in range(8):
        for j in range(n):
            p[c * n + j, j * 8 + c] = 1.0
    return p.astype(jnp.bfloat16)


def _s5_weight_specs(l):
    lay = lambda shape: pl.BlockSpec((None,) + shape, lambda *idx: (l, 0, 0))
    return [lay((GROUP_WIDTH, 2 * S5_STATE_WIDTH)), lay((2 * S5_POW, S5_STATE_WIDTH)),
            lay((S5_STATE_WIDTH, GROUP_WIDTH)), lay((S5_STATE_WIDTH, GROUP_WIDTH)),
            lay((1, GROUP_WIDTH)), lay((GROUP_WIDTH, GROUP_WIDTH)), lay((1, GROUP_WIDTH))]


def _s5_weights(W):
    return (W["s5_bbar"], W["s5_apow"], W["s5_c_re"], W["s5_c_im"], W["s5_d"], W["s5_glu_w"], W["s5_glu_b"])


def _s5_prompt(zs0, zs1, l, W, bsz, seq):
    t = MIX_TILE
    nt = seq // t
    w = S5_STATE_WIDTH
    st = pl.BlockSpec((None, 1, w), lambda b, i: (b, 0, 0))
    return dict(kernel=_s5_prompt_kernel, in_specs=[pl.BlockSpec((t, 128), lambda b, i: (b * nt + i, 0))] * 2 + _s5_weight_specs(l)
        + [pl.BlockSpec((t, t), lambda b, i: (0, 0))],
        out_specs=[pl.BlockSpec((t, GROUP_WIDTH), lambda b, i: (b * nt + i, 0)), st, st],
        out_shape=[jax.ShapeDtypeStruct((bsz * seq, GROUP_WIDTH), BF16),
                   jax.ShapeDtypeStruct((bsz, 1, w), F32), jax.ShapeDtypeStruct((bsz, 1, w), F32)],
                scratch=[pltpu.VMEM((t, w), F32), pltpu.VMEM((t, w), F32),
                        pltpu.VMEM((1, w), F32), pltpu.VMEM((1, w), F32)],
                args=[zs0, zs1, *_s5_weights(W), _s5_unpermute(t)])


def _s5_decode_kernel(u0_ref, u1_ref, x0r_ref, x0i_ref, bbar_ref, apow_ref, cre_ref, cim_ref, d_ref, gw_ref, gb_ref,
                      o_ref, xr_ref, xi_ref):
    rows = u0_ref.shape[0]
    w = S5_STATE_WIDTH
    u = jnp.concatenate([u0_ref[...], u1_ref[...]], axis=1)
    bu = _dot(u.astype(BF16), bbar_ref[...])
    first = (lax.broadcasted_iota(jnp.int32, (rows, 1), 0) & (DEC_LEN - 1)) == 0
    ar = apow_ref[0:1, :]
    ai = apow_ref[S5_POW:S5_POW + 1, :]
    x0r = x0r_ref[...]
    x0i = x0i_ref[...]
    br = bu[:, 0:w] + jnp.where(first, ar * x0r - ai * x0i, 0.0)
    bi = bu[:, w:] + jnp.where(first, ar * x0i + ai * x0r, 0.0)
    xr, xi = _s5_local_scan(br.reshape(rows // 8, 8, w), bi.reshape(rows // 8, 8, w), apow_ref, DEC_LEN)
    xr = xr.reshape(rows, w)
    xi = xi.reshape(rows, w)
    xr_ref[...] = xr
    xi_ref[...] = xi
    o_ref[...] = _s5_output(xr, xi, u, cre_ref, cim_ref, d_ref, gw_ref, gb_ref).astype(BF16)


def _s5_decode(zs0, zs1, x0r, x0i, l, W):
    rows = zs0.shape[0]
    rb = DEC_ROWS
    w = S5_STATE_WIDTH
    blk = lambda width: pl.BlockSpec((rb, width), lambda i: (i, 0))
    return dict(kernel=_s5_decode_kernel,
                in_specs=[blk(128), blk(128), blk(w), blk(w)] + _s5_weight_specs(l),
                out_specs=[blk(GROUP_WIDTH), blk(w), blk(w)],
                out_shape=[jax.ShapeDtypeStruct((rows, GROUP_WIDTH), BF16),
                           jax.ShapeDtypeStruct((rows, w), F32), jax.ShapeDtypeStruct((rows, w), F32)],
                aliases={}, scratch=[], args=[zs0, zs1, x0r, x0i, *_s5_weights(W)])


def _swa_prompt_kernel(sink_ref, za_ref, zprev_ref, bias_ref, o_ref):
    i = pl.program_id(1)
    w = SWA_WINDOW
    nblk = za_ref.shape[0] // w
    kfull = jnp.concatenate([zprev_ref[:, 256:384], za_ref[:, 256:384]], axis=0).astype(BF16)
    vfull = jnp.concatenate([zprev_ref[:, 384:512], za_ref[:, 384:512]], axis=0).astype(BF16)
    lane_half = lax.broadcasted_iota(jnp.int32, (1, w), 1) >> 6
    col = lax.broadcasted_iota(jnp.int32, (1, 2 * w), 1)
    no_prev = (col < w) & (i == 0)
    for n in range(nblk):
        kk = kfull[n * w:(n + 2) * w]
        vv = vfull[n * w:(n + 2) * w]
        for j in range(2):
            qj = za_ref[n * w:(n + 1) * w, j * w:(j + 1) * w]
            on_j = lane_half == j
            oj = jnp.zeros((w, w), F32)
            for g in range(2):
                h = j * 2 + g
                qa = qj if g == j else pltpu.roll(qj, HEAD_DIM, axis=1)
                sc = _dot_nt(jnp.where(on_j, qa, 0.0).astype(BF16), kk) * K_SCALE + bias_ref[h]
                if n == 0:
                    sc = jnp.where(no_prev, MASK_VALUE, sc)
                sink = sink_ref[h]
                m = jnp.maximum(jnp.max(sc, axis=-1, keepdims=True), sink)
                p = jnp.exp(sc - m)
                den = jnp.sum(p, axis=-1, keepdims=True) + jnp.exp(sink - m)
                og = jnp.where(on_j, _dot(p.astype(BF16), vv) / den, 0.0)
                oj = oj + (og if g == j else pltpu.roll(og, HEAD_DIM, axis=1))
            o_ref[n * w:(n + 1) * w, j * w:(j + 1) * w] = oj.astype(BF16)


def _swa_prompt(za, sinks, bsz, seq):
    t = MIX_TILE
    nt = seq // t
    per = t // SWA_WINDOW
    nb = seq // SWA_WINDOW
    bias = _swa_prompt_bias()
    return dict(kernel=_swa_prompt_kernel, in_specs=[pl.BlockSpec(memory_space=pltpu.SMEM),
                  pl.BlockSpec((t, 512), lambda b, i: (b * nt + i, 0)),
                  pl.BlockSpec((SWA_WINDOW, 512), lambda b, i: (b * nb + jnp.maximum(i * per - 1, 0), 0)),
                  pl.BlockSpec(bias.shape, lambda b, i: (0, 0, 0))],
                out_specs=pl.BlockSpec((t, GROUP_WIDTH), lambda b, i: (b * nt + i, 0)),
                out_shape=jax.ShapeDtypeStruct((bsz * seq, GROUP_WIDTH), BF16),
                scratch=[],
                args=[sinks, za, za, bias])


def _swa_decode_kernel(*refs, aliased):
    sink_ref, za_ref, kc_ref, vc_ref, bc_ref, bn_ref = refs[:6]
    o_ref, kn_ref, vn_ref, os_scr = refs[6 + (2 if aliased else 0):]
    rows = za_ref.shape[0]
    nseq = rows // DEC_LEN
    w = SWA_WINDOW
    keep = w - DEC_LEN
    lane_half = lax.broadcasted_iota(jnp.int32, (1, w), 1) >> 6
    sub8 = lax.broadcasted_iota(jnp.int32, (8, 1), 0)
    rhead = lax.broadcasted_iota(jnp.int32, (32, 1), 0) >> 3
    sink = jnp.zeros((32, 1), F32)
    for h in range(N_HEADS):
        sink = jnp.where(rhead == h, sink_ref[h], sink)

    def body(p, carry):
        r8 = pl.multiple_of(p * 8, 8)
        k8 = za_ref[pl.ds(r8, 8), 256:384]
        v8 = za_ref[pl.ds(r8, 8), 384:512]
        pieces = []
        for j in range(2):
            qj = za_ref[pl.ds(r8, 8), j * w:(j + 1) * w]
            for g in range(2):
                qa = qj if g == j else pltpu.roll(qj, HEAD_DIM, axis=1)
                pieces.append(jnp.where(lane_half == j, qa, 0.0))
        qs = jnp.concatenate(pieces, axis=0).astype(BF16)
        o8 = None
        for par in range(2):
            o_par = one_sequence(2 * p + par, par, k8, v8, qs)
            o8 = o_par if par == 0 else jnp.where(sub8 < DEC_LEN, o8, o_par)
        os_scr[pl.ds(r8, 8), :] = o8
        return carry

    def one_sequence(s, par, k8, v8, qs):
        kn_ref[s, 0:keep, :] = kc_ref[s, DEC_LEN:w, :]
        vn_ref[s, 0:keep, :] = vc_ref[s, DEC_LEN:w, :]
        kn_ref[s, keep:w, :] = k8[par * DEC_LEN:(par + 1) * DEC_LEN, :]
        vn_ref[s, keep:w, :] = v8[par * DEC_LEN:(par + 1) * DEC_LEN, :]
        sc_c = _dot_nt(qs, kc_ref[s].astype(BF16)) * K_SCALE + bc_ref[par]
        sc_n = _dot_nt(qs, k8.astype(BF16)) * K_SCALE + bn_ref[par]
        m = jnp.maximum(jnp.maximum(jnp.max(sc_c, axis=-1, keepdims=True),
                                    jnp.max(sc_n, axis=-1, keepdims=True)), sink)
        p_c = jnp.exp(sc_c - m)
        p_n = jnp.exp(sc_n - m)
        den = jnp.sum(p_c, axis=-1, keepdims=True) + jnp.sum(p_n, axis=-1, keepdims=True) + jnp.exp(sink - m)
        o32 = (_dot(p_c.astype(BF16), vc_ref[s].astype(BF16)) + _dot(p_n.astype(BF16), v8.astype(BF16))) / den
        tiles = []
        for j in range(2):
            oj = jnp.zeros((8, w), F32)
            for g in range(2):
                h = j * 2 + g
                og = jnp.where(lane_half == j, o32[h * 8:(h + 1) * 8, :], 0.0)
                oj = oj + (og if g == j else pltpu.roll(og, HEAD_DIM, axis=1))
            tiles.append(oj)
        return jnp.concatenate(tiles, axis=1)

    lax.fori_loop(0, nseq // 2, body, 0, unroll=8)
    o_ref[...] = os_scr[...].astype(BF16)


def _swa_decode(za, kc, vc, prev_k, prev_v, l, sinks):
    rows = za.shape[0]
    rb = DEC_ROWS
    nseq = rb // DEC_LEN
    bc, bn = _swa_decode_bias()
    cache = pl.BlockSpec((None, nseq, SWA_WINDOW, 128), lambda i: (l, i, 0, 0))
    in_specs = [pl.BlockSpec(memory_space=pltpu.SMEM),
                pl.BlockSpec((rb, 512), lambda i: (i, 0)), cache, cache,
                pl.BlockSpec(bc.shape, lambda i: (0, 0, 0)), pl.BlockSpec(bn.shape, lambda i: (0, 0, 0))]
    args = [sinks, za, kc, vc, bc, bn]
    aliases = {}
    if prev_k is not None:
        aliases = {len(args): 1, len(args) + 1: 2}
        in_specs = in_specs + [pl.BlockSpec(memory_space=pl.ANY)] * 2
        args = args + [prev_k, prev_v]
    return dict(kernel=functools.partial(_swa_decode_kernel, aliased=prev_k is not None),
                in_specs=in_specs,
                out_specs=[pl.BlockSpec((rb, GROUP_WIDTH), lambda i: (i, 0)), cache, cache],
                out_shape=[jax.ShapeDtypeStruct((rows, GROUP_WIDTH), BF16),
                           jax.ShapeDtypeStruct(kc.shape, F32), jax.ShapeDtypeStruct(vc.shape, F32)],
                aliases=aliases, scratch=[pltpu.VMEM((rb, GROUP_WIDTH), F32)], args=args)


def _as_list(x):
    return list(x) if isinstance(x, (list, tuple)) else [x]


def _merged_kernel(*refs, parts):
    n_in = [len(p["in_specs"]) for p in parts]
    n_out = [len(_as_list(p["out_specs"])) for p in parts]
    n_scr = [len(p["scratch"]) for p in parts]
    ins, outs, scr = refs[:sum(n_in)], refs[sum(n_in):sum(n_in) + sum(n_out)], refs[sum(n_in) + sum(n_out):]
    i = o = c = 0
    for p, ni, no, nc in zip(parts, n_in, n_out, n_scr):
        p["kernel"](*ins[i:i + ni], *outs[o:o + no], *scr[c:c + nc])
        i, o, c = i + ni, o + no, c + nc


def _run_merged(parts, grid, name):
    aliases = {}
    i = o = 0
    for p in parts:
        for src, dst in p.get("aliases", {}).items():
            aliases[i + src] = o + dst
        i += len(p["in_specs"])
        o += len(_as_list(p["out_specs"]))
    return pl.pallas_call(
        functools.partial(_merged_kernel, parts=parts),
        grid=grid,
        in_specs=[sp for p in parts for sp in p["in_specs"]],
        out_specs=[sp for p in parts for sp in _as_list(p["out_specs"])],
        out_shape=[sh for p in parts for sh in _as_list(p["out_shape"])],
        input_output_aliases=aliases,
        scratch_shapes=[sc for p in parts for sc in p["scratch"]],
        compiler_params=_params(("arbitrary",) * len(grid)),
        name=name,
    )(*[a for p in parts for a in p["args"]])


def _prompt_mixers(zr, zs0, zs1, zh, za, l, W, bsz, seq):
    parts = [_ret_prompt(zr, l, W, bsz, seq), _s5_prompt(zs0, zs1, l, W, bsz, seq),
             _hgrn_prompt(zh, l, W, bsz, seq), _swa_prompt(za, W["swa_sinks"][l], bsz, seq)]
    return _run_merged(parts, (bsz, seq // MIX_TILE), "prompt_mixers")


def _prepare_weights(w_in, w_out, norm_pre_mix, norm_post_mix, norm_pre_ffn, norm_post_ffn, ret_norm_w,
                     s5_a_re, s5_a_im, s5_log_step, s5_b_re, s5_b_im, s5_c_re, s5_c_im, s5_d, s5_glu_w,
                     s5_glu_b, hgrn_lower_bounds, hgrn_norm_w, swa_sinks, ffn_w_gate, ffn_w_up, ffn_w_down):
    eye = jnp.eye(16, dtype=F32)
    b_bd = lambda b: jnp.einsum("lgpc,gh->lgchp", b, eye).reshape(DEPTH, GROUP_WIDTH, S5_STATE_WIDTH)
    c_bd = lambda c: jnp.einsum("lgcp,gh->lgphc", c, eye).reshape(DEPTH, S5_STATE_WIDTH, GROUP_WIDTH)
    vec = lambda a, n: a.reshape(DEPTH, 1, n)
    bbar, apow, lb = _prep(vec(s5_a_re, S5_STATE_WIDTH), vec(s5_a_im, S5_STATE_WIDTH),
                           jnp.repeat(s5_log_step, 64, axis=1).reshape(DEPTH, 1, S5_STATE_WIDTH),
                           b_bd(s5_b_re), b_bd(s5_b_im), hgrn_lower_bounds)
    return dict(
        w_in=w_in.astype(BF16), w_out=w_out.astype(BF16),
        norm_pre_mix=vec(norm_pre_mix, D_MODEL), norm_post_mix=vec(norm_post_mix, D_MODEL),
        norm_pre_ffn=vec(norm_pre_ffn, D_MODEL), norm_post_ffn=vec(norm_post_ffn, D_MODEL),
        ret_norm_w=vec(ret_norm_w, GROUP_WIDTH), hgrn_norm_w=vec(hgrn_norm_w, GROUP_WIDTH), hgrn_lb=lb,
        s5_bbar=bbar, s5_apow=apow, s5_c_re=c_bd(s5_c_re).astype(BF16), s5_c_im=c_bd(s5_c_im).astype(BF16),
        s5_d=vec(s5_d, GROUP_WIDTH), s5_glu_w=s5_glu_w.astype(BF16), s5_glu_b=vec(s5_glu_b, GROUP_WIDTH),
        swa_sinks=swa_sinks,
        ffn_w_gate=ffn_w_gate.astype(BF16), ffn_w_up=ffn_w_up.astype(BF16), ffn_w_down=ffn_w_down.astype(BF16))


def _prompt_trunk(x, W):
    bsz, seq, _ = x.shape
    x = x.reshape(bsz * seq, D_MODEL)
    acc = [[] for _ in range(6)]
    for l in range(DEPTH):
        zr, zs0, zs1, zh, za = _in_proj(x, l, W)
        o_ret, s_ret, o_s5, s_re, s_im, o_h, s_ht, o_a = _prompt_mixers(zr, zs0, zs1, zh, za, l, W, bsz, seq)
        x = _out_ffn(x, (o_ret, o_s5, o_h, o_a), l, W)
        kv = za.reshape(bsz, seq, 512)[:, seq - SWA_WINDOW:, 256:]
        new = (s_ret, s_re.reshape(bsz, 16, 64), s_im.reshape(bsz, 16, 64), jnp.swapaxes(s_ht, -1, -2),
               kv[..., 0:128].reshape(bsz, SWA_WINDOW, 2, HEAD_DIM),
               kv[..., 128:].reshape(bsz, SWA_WINDOW, 2, HEAD_DIM))
        for a, n in zip(acc, new):
            a.append(n)
    return x.reshape(bsz, seq, D_MODEL), [jnp.stack(a) for a in acc]


def _decode_trunk(x, W, states):
    s_ret, s5_re, s5_im, s_hgrn, buf_k, buf_v = states
    bsz, seq, _ = x.shape
    w = S5_STATE_WIDTH
    x = x.reshape(bsz * seq, D_MODEL)
    st_ret = s_ret.reshape(DEPTH, bsz, GROUP_WIDTH, HEAD_DIM)
    st_hgrn = jnp.swapaxes(s_hgrn, -1, -2).reshape(DEPTH, bsz, GROUP_WIDTH, HEAD_DIM)
    kc = buf_k.reshape(DEPTH, bsz, SWA_WINDOW, 128)
    vc = buf_v.reshape(DEPTH, bsz, SWA_WINDOW, 128)
    n_ret = n_hgrn = n_k = n_v = None
    s5_new = [[], []]
    for l in range(DEPTH):
        zr, zs0, zs1, zh, za = _in_proj(x, l, W)
        x0r = jnp.repeat(s5_re[l].reshape(bsz, w), DEC_LEN, axis=0)
        x0i = jnp.repeat(s5_im[l].reshape(bsz, w), DEC_LEN, axis=0)
        parts = [_rec_decode(zr, st_ret, n_ret, l, W, False), _s5_decode(zs0, zs1, x0r, x0i, l, W),
                 _rec_decode(zh, st_hgrn, n_hgrn, l, W, True),
                 _swa_decode(za, kc, vc, n_k, n_v, l, W["swa_sinks"][l])]
        o_ret, n_ret, o_s5, xr, xi, o_h, n_hgrn, o_a, n_k, n_v = _run_merged(
            parts, (x.shape[0] // DEC_ROWS,), "decode_mixers")
        x = _out_ffn(x, (o_ret, o_s5, o_h, o_a), l, W)
        for acc, xs in zip(s5_new, (xr, xi)):
            acc.append(xs.reshape(bsz, DEC_LEN, w)[:, DEC_LEN - 1].reshape(bsz, 16, 64))
    heads = (DEPTH, bsz, N_HEADS, HEAD_DIM, HEAD_DIM)
    new = [n_ret.reshape(heads), jnp.stack(s5_new[0]), jnp.stack(s5_new[1]),
           jnp.swapaxes(n_hgrn.reshape(heads), -1, -2),
           n_k.reshape(buf_k.shape), n_v.reshape(buf_v.shape)]
    return x.reshape(bsz, seq, D_MODEL), new


def kernel(x_prompt, x_sample, state_ret, state_s5_re, state_s5_im, state_hgrn, cache_swa_k, cache_swa_v,
           w_in, w_out, norm_pre_mix, norm_post_mix, norm_pre_ffn, norm_post_ffn, ret_norm_w,
           s5_a_re, s5_a_im, s5_log_step, s5_b_re, s5_b_im, s5_c_re, s5_c_im, s5_d, s5_glu_w, s5_glu_b,
           hgrn_lower_bounds, hgrn_norm_w, swa_sinks, ffn_w_gate, ffn_w_up, ffn_w_down):
    W = _prepare_weights(w_in, w_out, norm_pre_mix, norm_post_mix, norm_pre_ffn, norm_post_ffn, ret_norm_w,
                         s5_a_re, s5_a_im, s5_log_step, s5_b_re, s5_b_im, s5_c_re, s5_c_im, s5_d, s5_glu_w,
                         s5_glu_b, hgrn_lower_bounds, hgrn_norm_w, swa_sinks, ffn_w_gate, ffn_w_up, ffn_w_down)
    y_prompt, p_states = _prompt_trunk(x_prompt, W)
    y_sample, s_states = _decode_trunk(x_sample, W, (state_ret, state_s5_re, state_s5_im, state_hgrn,
                                                     cache_swa_k, cache_swa_v))
    return (y_prompt, y_sample, *p_states, *s_states)
```
